```python
import math
import jax
import jax.numpy as jnp
from jax import lax
import numpy as np

D_MODEL = 4096
BATCH = 4
SEQ = 2048
DEPTH = 1

HEAD_DIM = 128
N_SB_HEADS = 16
N_DIL_HEADS = 16
SB_WIDTH = N_SB_HEADS * HEAD_DIM
DIL_WIDTH = N_DIL_HEADS * HEAD_DIM
MIX_WIDTH = SB_WIDTH + DIL_WIDTH
SPLIT_SIZES = (SB_WIDTH, SB_WIDTH, SB_WIDTH, SB_WIDTH, DIL_WIDTH, DIL_WIDTH, DIL_WIDTH, DIL_WIDTH)
IN_COLS = sum(SPLIT_SIZES)
SPLIT_POINTS = tuple(sum(SPLIT_SIZES[:i + 1]) for i in range(len(SPLIT_SIZES) - 1))
Q_BLOCK = 128
DIL_PAIRS = ((128, 1), (512, 4), (2048, 16))
ALIBI_MAX_BIAS = 8.0
EPS = 1e-6

kernel_name = 'hybrid_stickbreak_dilated_block'


def rmsnorm(x, g):
    xf = x.astype(jnp.float32)
    y = xf * lax.rsqrt(jnp.mean(xf * xf, axis=-1, keepdims=True) + EPS)
    return (y * g.astype(jnp.float32)).astype(x.dtype)


def split_heads(t, n_heads):
    b, s, _ = t.shape
    return t.reshape(b, s, n_heads, HEAD_DIM).transpose(0, 2, 1, 3)


def head_rmsnorm_merge(y, g):
    b, h, s, d = y.shape
    yf = y.astype(jnp.float32)
    yf = yf * lax.rsqrt(jnp.mean(yf * yf, axis=-1, keepdims=True) + EPS)
    return yf.transpose(0, 2, 1, 3).reshape(b, s, h * d) * g.astype(jnp.float32)


def alibi_slopes(n_heads):
    return jnp.exp2(-ALIBI_MAX_BIAS * jnp.arange(1, n_heads + 1, dtype=jnp.float32) / n_heads)


def stick_breaking_attention(q, k, v):
    b, h, s, d = q.shape
    nb = s // Q_BLOCK
    inv_sqrt_d = 1.0 / math.sqrt(d)
    kf = k.astype(jnp.float32)
    vf = v.astype(jnp.float32)
    qb = q.astype(jnp.float32).reshape(b, h, nb, Q_BLOCK, d).transpose(2, 0, 1, 3, 4)
    k_pos = jnp.arange(s)

    def one_block(args):
        qi, blk = args
        z = jnp.einsum('bhqd,bhkd->bhqk', qi, kf) * inv_sqrt_d
        q_pos = blk * Q_BLOCK + jnp.arange(Q_BLOCK)
        mask = k_pos[None, :] < q_pos[:, None]
        log_beta = jax.nn.log_sigmoid(z)
        log_one_minus = jnp.where(mask, jax.nn.log_sigmoid(-z), 0.0)
        suffix = lax.cumsum(log_one_minus, axis=3, reverse=True) - log_one_minus
        a = jnp.where(mask, jnp.exp(log_beta + suffix), 0.0)
        return jnp.einsum('bhqk,bhkd->bhqd', a, vf)

    out = lax.map(one_block, (qb, jnp.arange(nb)))
    return out.transpose(1, 2, 0, 3, 4).reshape(b, h, s, d)


def dilated_branch(q, k, v, slopes, window, dilation):
    b, h, s, d = q.shape
    n = window // dilation
    L = s // dilation
    Lp = -(-L // n) * n
    nb = Lp // n
    inv_sqrt_d = 1.0 / math.sqrt(d)

    def to_local(t):
        t = t.astype(jnp.float32).reshape(b, h, L, dilation, d).transpose(0, 1, 3, 2, 4)
        t = jnp.pad(t, ((0, 0), (0, 0), (0, 0), (0, Lp - L), (0, 0)))
        return t.reshape(b, h, dilation, nb, n, d)

    def with_prev(t):
        prev = jnp.pad(t, ((0, 0), (0, 0), (0, 0), (1, 0), (0, 0), (0, 0)))[:, :, :, :-1]
        return jnp.concatenate([prev, t], axis=4)

    ql = to_local(q)
    kw = with_prev(to_local(k))
    vw = with_prev(to_local(v))
    qi = jnp.arange(n)[:, None]
    ki = jnp.arange(2 * n)[None, :] - n
    steps = qi - ki
    blk = jnp.arange(nb)[:, None, None]
    valid = (steps >= 0)[None] & (steps <= n)[None] & ((blk * n + ki[None]) >= 0)
    dist = (steps * dilation).astype(jnp.float32)
    sc = jnp.einsum('bhrnqd,bhrnkd->bhrnqk', ql, kw) * inv_sqrt_d
    sc = sc - slopes[None, :, None, None, None, None] * dist
    sc = jnp.where(valid, sc, -jnp.inf)
    m = jnp.max(sc, axis=-1, keepdims=True)
    p = jnp.exp(sc - m)
    den = jnp.sum(p, axis=-1, keepdims=True)
    num = jnp.einsum('bhrnqk,bhrnkd->bhrnqd', p, vw)

    def from_local(t):
        x_dim = t.shape[-1]
        t = t.reshape(b, h, dilation, Lp, x_dim)[:, :, :, :L]
        return t.transpose(0, 1, 3, 2, 4).reshape(b, h, s, x_dim)

    return from_local(num), from_local(den), from_local(m)


def dilated_attention(q, k, v):
    slopes = alibi_slopes(q.shape[1])
    parts = [dilated_branch(q, k, v, slopes, w, r) for (w, r) in DIL_PAIRS]
    m_all = parts[0][2]
    for part in parts[1:]:
        m_all = jnp.maximum(m_all, part[2])
    num = sum(pn * jnp.exp(pm - m_all) for (pn, _, pm) in parts)
    den = sum(pd * jnp.exp(pm - m_all) for (_, pd, pm) in parts)
    return num / den


def setup_inputs(seed: int = 0) -> dict:
    key = jax.random.key(seed)
    ks = jax.random.split(key, 10)
    x = jax.random.normal(ks[0], (BATCH, SEQ, D_MODEL), jnp.float32)
    c = jax.random.normal(ks[1], (BATCH, D_MODEL), jnp.float32)
    w_ada = jax.random.normal(ks[2], (DEPTH, D_MODEL, 3 * D_MODEL), jnp.float32) * D_MODEL ** -0.5
    b_ada = 0.01 * jax.random.normal(ks[3], (DEPTH, 3 * D_MODEL), jnp.float32)
    g_norm = 1.0 + 0.1 * jax.random.normal(ks[4], (DEPTH, D_MODEL), jnp.float32)
    w_in = jax.random.normal(ks[5], (DEPTH, D_MODEL, IN_COLS), jnp.float32) * D_MODEL ** -0.5
    g_sb = 1.0 + 0.1 * jax.random.normal(ks[6], (DEPTH, SB_WIDTH), jnp.float32)
    g_dil = 1.0 + 0.1 * jax.random.normal(ks[7], (DEPTH, DIL_WIDTH), jnp.float32)
    w_out = jax.random.normal(ks[8], (DEPTH, MIX_WIDTH, D_MODEL), jnp.float32) * MIX_WIDTH ** -0.5
    g_final = 1.0 + 0.1 * jax.random.normal(ks[9], (D_MODEL,), jnp.float32)
    return {'x': x, 'c': c, 'w_ada': w_ada, 'b_ada': b_ada, 'g_norm': g_norm,
            'w_in': w_in, 'g_sb': g_sb, 'g_dil': g_dil, 'w_out': w_out, 'g_final': g_final}


def reference(x, c, w_ada, b_ada, g_norm, w_in, g_sb, g_dil, w_out, g_final):
    cs = jax.nn.silu(c.astype(jnp.float32))
    for layer in range(DEPTH):
        mod = cs @ w_ada[layer].astype(jnp.float32) + b_ada[layer].astype(jnp.float32)
        shift, scale, gate = jnp.split(mod, 3, axis=-1)
        h = rmsnorm(x, g_norm[layer]).astype(jnp.float32) * (1.0 + scale[:, None, :]) + shift[:, None, :]
        proj = jnp.einsum('bsd,de->bse', h.astype(x.dtype), w_in[layer])
        sb_q, sb_k, sb_v, sb_z, dl_q, dl_k, dl_v, dl_z = jnp.split(proj, SPLIT_POINTS, axis=-1)
        y_sb = stick_breaking_attention(split_heads(sb_q, N_SB_HEADS), split_heads(sb_k, N_SB_HEADS),
                                        split_heads(sb_v, N_SB_HEADS))
        y_sb = head_rmsnorm_merge(y_sb, g_sb[layer]) * jax.nn.silu(sb_z.astype(jnp.float32))
        y_dl = dilated_attention(split_heads(dl_q, N_DIL_HEADS), split_heads(dl_k, N_DIL_HEADS),
                                 split_heads(dl_v, N_DIL_HEADS))
        y_dl = head_rmsnorm_merge(y_dl, g_dil[layer]) * jax.nn.silu(dl_z.astype(jnp.float32))
        y = jnp.concatenate([y_sb, y_dl], axis=-1).astype(x.dtype)
        out = jnp.einsum('bse,ed->bsd', y, w_out[layer])
        x = x + (gate[:, None, :] * out.astype(jnp.float32)).astype(x.dtype)
    return rmsnorm(x, g_final)
```

```python
import functools
import math

import jax
import jax.numpy as jnp
from jax import lax
from jax.experimental import pallas as pl
from jax.experimental.pallas import tpu as pltpu

HEAD_DIM = 128
EPS = 1e-6
LOG2E = 1.4426950408889634
ALIBI_MAX_BIAS = 8.0
DIL_PAIRS = ((128, 1), (512, 4), (2048, 16))
DIL_BLOCK = 128
MASK_BIAS = -1e30
V7X_VMEM_BYTES = 64 * 1024 * 1024
VMEM_LIMIT = V7X_VMEM_BYTES - 8 * 1024 * 1024

F32 = jnp.float32
BF16 = jnp.bfloat16


def _params(*sem):
    return pltpu.CompilerParams(dimension_semantics=sem, vmem_limit_bytes=VMEM_LIMIT)


def _pick(n, pref):
    t = min(pref, n)
    while n % t:
        t //= 2
    return t


def _silu(x):
    return x / (1.0 + jnp.exp(-x))


def _ada_kernel(c_ref, w_ref, b_ref, o_ref):
    cs = _silu(c_ref[...])
    acc = jnp.dot(cs.astype(BF16), w_ref[...].astype(BF16), preferred_element_type=F32)
    o_ref[...] = acc + b_ref[...]


def _ada(c_pad, w, b):
    m, d = c_pad.shape
    n = w.shape[1]
    tn = _pick(n, 512)
    return pl.pallas_call(
        _ada_kernel,
        grid=(n // tn,),
        in_specs=[
            pl.BlockSpec((m, d), lambda j: (0, 0)),
            pl.BlockSpec((d, tn), lambda j: (0, j)),
            pl.BlockSpec((1, tn), lambda j: (0, j)),
        ],
        out_specs=pl.BlockSpec((m, tn), lambda j: (0, j)),
        out_shape=jax.ShapeDtypeStruct((m, n), F32),
        compiler_params=_params("arbitrary"),
        name="ada_ln",
    )(c_pad, w, b)


def _norm_mod_kernel(x_ref, g_ref, scale_ref, shift_ref, h_ref):
    x = x_ref[...]
    ms = jnp.mean(x * x, axis=-1, keepdims=True)
    y = x * lax.rsqrt(ms + EPS) * g_ref[...]
    h_ref[...] = (y * (1.0 + scale_ref[...]) + shift_ref[...]).astype(BF16)


def _norm_mod(x, g, scale, shift):
    b, s, d = x.shape
    ts = _pick(s, 256)
    return pl.pallas_call(
        _norm_mod_kernel,
        grid=(b, s // ts),
        in_specs=[
            pl.BlockSpec((None, ts, d), lambda i, j: (i, j, 0)),
            pl.BlockSpec((1, d), lambda i, j: (0, 0)),
            pl.BlockSpec((None, 1, d), lambda i, j: (i, 0, 0)),
            pl.BlockSpec((None, 1, d), lambda i, j: (i, 0, 0)),
        ],
        out_specs=pl.BlockSpec((None, ts, d), lambda i, j: (i, j, 0)),
        out_shape=jax.ShapeDtypeStruct((b, s, d), BF16),
        compiler_params=_params("arbitrary", "arbitrary"),
        name="norm_mod",
    )(x, g, scale, shift)


def _inproj_kernel(h_ref, w_ref, cs_ref, o_ref, wb_ref):
    @pl.when(pl.program_id(1) == 0)
    def _():
        wb_ref[...] = w_ref[...].astype(BF16)

    acc = jnp.dot(h_ref[...], wb_ref[...], preferred_element_type=F32)
    o_ref[...] = (acc * cs_ref[...]).astype(BF16)


def _inproj(h2, w, colscale):
    m, d = h2.shape
    n = w.shape[1]
    tm = _pick(m, 512)
    tn = _pick(n, 512)
    return pl.pallas_call(
        _inproj_kernel,
        grid=(n // tn, m // tm),
        in_specs=[
            pl.BlockSpec((tm, d), lambda j, i: (i, 0)),
            pl.BlockSpec((d, tn), lambda j, i: (0, j)),
            pl.BlockSpec((1, tn), lambda j, i: (0, j)),
        ],
        out_specs=pl.BlockSpec((tm, tn), lambda j, i: (i, j)),
        out_shape=jax.ShapeDtypeStruct((m, n), BF16),
        scratch_shapes=[pltpu.VMEM((d, tn), BF16)],
        compiler_params=_params("arbitrary", "arbitrary"),
        name="in_proj",
    )(h2, w, colscale)


def _head_epilogue(o, g, zg):
    ms = jnp.mean(o * o, axis=-1, keepdims=True)
    return o * lax.rsqrt(ms + EPS) * g * _silu(zg)


def _sb_kernel(q_ref, k_ref, v_ref, z_ref, g_ref, o_ref, *, seq, tile):
    row = lax.broadcasted_iota(jnp.int32, (tile, tile), 0)
    col = lax.broadcasted_iota(jnp.int32, (tile, tile), 1)
    causal = col < row
    upper = jnp.where(row > col, 1.0, 0.0).astype(BF16)

    def tile_step(q, k0, carry, acc, diag):
        k = k_ref[pl.ds(k0, tile), :]
        v = v_ref[pl.ds(k0, tile), :]
        s = lax.dot_general(q, k, (((1,), (1,)), ((), ())), preferred_element_type=F32)
        e = jnp.exp2(-jnp.abs(s))
        lb = jnp.minimum(s, 0.0) - jnp.log(1.0 + e) * LOG2E
        lom = lb - s
        if diag:
            lom = jnp.where(causal, lom, 0.0)
        suffix = jnp.dot(lom.astype(BF16), upper, preferred_element_type=F32)
        a = jnp.exp2(lb + suffix + carry)
        if diag:
            a = jnp.where(causal, a, 0.0)
        acc = acc + jnp.dot(a.astype(BF16), v, preferred_element_type=F32)
        carry = carry + jnp.sum(lom, axis=1, keepdims=True)
        return carry, acc

    for qi in range(seq // tile):
        q0 = qi * tile
        q = q_ref[pl.ds(q0, tile), :]
        carry = jnp.zeros((tile, 1), F32)
        acc = jnp.zeros((tile, HEAD_DIM), F32)
        carry, acc = tile_step(q, q0, carry, acc, True)

        def body(i, ca, q=q, qi=qi):
            k0 = pl.multiple_of((qi - 1 - i) * tile, tile)
            return tile_step(q, k0, ca[0], ca[1], False)

        if qi:
            carry, acc = lax.fori_loop(0, qi, body, (carry, acc))
        zg = z_ref[pl.ds(q0, tile), :].astype(F32)
        o_ref[pl.ds(q0, tile), :] = _head_epilogue(acc, g_ref[...], zg).astype(BF16)


def _sb_attention(proj, g_sb, n_heads, col0):
    b, s, _ = proj.shape
    tile = _pick(s, 256)

    def spec(off):
        return pl.BlockSpec((None, s, HEAD_DIM), lambda i, h: (i, 0, col0 + off * n_heads + h))

    return pl.pallas_call(
        functools.partial(_sb_kernel, seq=s, tile=tile),
        grid=(b, n_heads),
        in_specs=[spec(0), spec(1), spec(2), spec(3),
                  pl.BlockSpec((1, HEAD_DIM), lambda i, h: (0, h))],
        out_specs=pl.BlockSpec((None, s, HEAD_DIM), lambda i, h: (i, 0, h)),
        out_shape=jax.ShapeDtypeStruct((b, s, n_heads * HEAD_DIM), BF16),
        compiler_params=_params("arbitrary", "arbitrary"),
        name="sb_attn",
    )(proj, proj, proj, proj, g_sb)


def _dil_kernel(q_ref, k_ref, v_ref, z_ref, g_ref, slope_ref, o_ref,
                qf, kf, vf, qp, kp, vop, vo, num_s, den_s, max_s, *, seq):
    n = DIL_BLOCK
    qi = lax.broadcasted_iota(jnp.int32, (n, 2 * n), 0)
    ci = lax.broadcasted_iota(jnp.int32, (n, 2 * n), 1)
    steps = qi + n - ci
    valid = (steps >= 0) & (steps <= n)
    stepsf = steps.astype(F32)
    slope = slope_ref[...]
    slope2 = jnp.concatenate([slope, slope], axis=1)

    ones = jnp.ones((seq, HEAD_DIM), BF16)
    vo[:, :HEAD_DIM] = v_ref[...]
    vo[:, HEAD_DIM:] = ones
    vop[:, HEAD_DIM:] = ones

    def block(qsrc, ksrc, vsrc, row0, first, bias, out_start, stride, br):
        q = qsrc[pl.ds(row0, n), :]
        if first:
            k = ksrc[pl.ds(row0, n), :]
            vv = vsrc[pl.ds(row0, n), :]
            bias = bias[:, n:]
        else:
            k = ksrc[pl.ds(row0 - n, 2 * n), :]
            vv = vsrc[pl.ds(row0 - n, 2 * n), :]
        s = lax.dot_general(q, k, (((1,), (1,)), ((), ())), preferred_element_type=F32) + bias
        m = jnp.max(s, axis=1, keepdims=True)
        p = jnp.exp2(s - m)
        pv = jnp.dot(p.astype(BF16), vv, preferred_element_type=F32)
        if stride == 1:
            idx = pl.ds(out_start, n)
        else:
            idx = pl.ds(out_start, n, stride=stride)
        num_s[br, idx, :] = pv[:, :HEAD_DIM]
        den_s[br, idx, :] = pv[:, HEAD_DIM:]
        max_s[br, idx, :] = jnp.broadcast_to(m, (n, HEAD_DIM))

    for br, (window, r) in enumerate(DIL_PAIRS):
        bias = jnp.where(valid, stepsf * (-float(r)) * slope2, MASK_BIAS)
        seg_len = seq // r
        nblk = seg_len // n
        if r == 1:
            qsrc, ksrc, vsrc = q_ref, k_ref, vo
        else:
            if br == 1:
                qf[...] = q_ref[...].astype(F32)
                kf[...] = k_ref[...].astype(F32)
                vf[...] = v_ref[...].astype(F32)
            for res in range(r):
                sl = pl.ds(res, seg_len, stride=r)
                dst = pl.ds(res * seg_len, seg_len)
                qp[dst, :] = qf[sl, :].astype(BF16)
                kp[dst, :] = kf[sl, :].astype(BF16)
                vop[dst, :HEAD_DIM] = vf[sl, :].astype(BF16)
            qsrc, ksrc, vsrc = qp, kp, vop

        def seg_body(res, _, r=r, seg_len=seg_len, nblk=nblk, bias=bias, br=br,
                     qsrc=qsrc, ksrc=ksrc, vsrc=vsrc):
            base = pl.multiple_of(res * seg_len, n)
            block(qsrc, ksrc, vsrc, base, True, bias, res, r, br)

            def blk_body(i, _):
                row0 = pl.multiple_of(base + i * n, n)
                block(qsrc, ksrc, vsrc, row0, False, bias, r * i * n + res, r, br)
                return 0

            if nblk > 1:
                lax.fori_loop(1, nblk, blk_body, 0)
            return 0

        lax.fori_loop(0, r, seg_body, 0)

    chunk = _pick(seq, 256)
    for c0 in range(0, seq, chunk):
        rows = pl.ds(c0, chunk)
        m_all = jnp.maximum(jnp.maximum(max_s[0, rows, :], max_s[1, rows, :]), max_s[2, rows, :])
        num = jnp.zeros((chunk, HEAD_DIM), F32)
        den = jnp.zeros((chunk, HEAD_DIM), F32)
        for br in range(len(DIL_PAIRS)):
            w = jnp.exp2(max_s[br, rows, :] - m_all)
            num = num + num_s[br, rows, :] * w
            den = den + den_s[br, rows, :] * w
        o = num / den
        zg = z_ref[rows, :].astype(F32)
        o_ref[rows, :] = _head_epilogue(o, g_ref[...], zg).astype(BF16)


def _dil_attention(proj, g_dil, slopes, n_heads, col0):
    b, s, _ = proj.shape
    nbr = len(DIL_PAIRS)

    def spec(off):
        return pl.BlockSpec((None, s, HEAD_DIM), lambda i, h: (i, 0, col0 + off * n_heads + h))

    return pl.pallas_call(
        functools.partial(_dil_kernel, seq=s),
        grid=(b, n_heads),
        in_specs=[spec(0), spec(1), spec(2), spec(3),
                  pl.BlockSpec((1, HEAD_DIM), lambda i, h: (0, h)),
                  pl.BlockSpec((None, 1, HEAD_DIM), lambda i, h: (h, 0, 0))],
        out_specs=pl.BlockSpec((None, s, HEAD_DIM), lambda i, h: (i, 0, h)),
        out_shape=jax.ShapeDtypeStruct((b, s, n_heads * HEAD_DIM), BF16),
        scratch_shapes=[
            pltpu.VMEM((s, HEAD_DIM), F32), pltpu.VMEM((s, HEAD_DIM), F32), pltpu.VMEM((s, HEAD_DIM), F32),
            pltpu.VMEM((s, HEAD_DIM), BF16), pltpu.VMEM((s, HEAD_DIM), BF16),
            pltpu.VMEM((s, 2 * HEAD_DIM), BF16), pltpu.VMEM((s, 2 * HEAD_DIM), BF16),
            pltpu.VMEM((nbr, s, HEAD_DIM), F32), pltpu.VMEM((nbr, s, HEAD_DIM), F32),
            pltpu.VMEM((nbr, s, HEAD_DIM), F32),
        ],
        compiler_params=_params("arbitrary", "arbitrary"),
        name="dil_attn",
    )(proj, proj, proj, proj, g_dil, slopes)


def _outproj_kernel(ya_ref, yb_ref, wa_ref, wb_ref, x_ref, gate_ref, o_ref, wab_ref, wbb_ref):
    @pl.when(pl.program_id(1) == 0)
    def _():
        wab_ref[...] = wa_ref[...].astype(BF16)
        wbb_ref[...] = wb_ref[...].astype(BF16)

    acc = jnp.dot(ya_ref[...], wab_ref[...], preferred_element_type=F32)
    acc = acc + jnp.dot(yb_ref[...], wbb_ref[...], preferred_element_type=F32)
    o_ref[...] = x_ref[...] + gate_ref[...] * acc


def _outproj(y_sb, y_dl, w_out, x, gate):
    b, s, d = x.shape
    ka = y_sb.shape[2]
    kb = y_dl.shape[2]
    tm = _pick(s, 512)
    tn = _pick(d, 512)
    nm = s // tm
    return pl.pallas_call(
        _outproj_kernel,
        grid=(d // tn, b * nm),
        in_specs=[
            pl.BlockSpec((None, tm, ka), lambda j, i: (i // nm, i % nm, 0)),
            pl.BlockSpec((None, tm, kb), lambda j, i: (i // nm, i % nm, 0)),
            pl.BlockSpec((ka, tn), lambda j, i: (0, j)),
            pl.BlockSpec((kb, tn), lambda j, i: (ka // kb, j)),
            pl.BlockSpec((None, tm, tn), lambda j, i: (i // nm, i % nm, j)),
            pl.BlockSpec((None, 1, tn), lambda j, i: (i // nm, 0, j)),
        ],
        out_specs=pl.BlockSpec((None, tm, tn), lambda j, i: (i // nm, i % nm, j)),
        out_shape=jax.ShapeDtypeStruct((b, s, d), F32),
        scratch_shapes=[pltpu.VMEM((ka, tn), BF16), pltpu.VMEM((kb, tn), BF16)],
        compiler_params=_params("arbitrary", "arbitrary"),
        name="out_proj",
    )(y_sb, y_dl, w_out, w_out, x, gate)


def _final_norm_kernel(x_ref, g_ref, o_ref):
    x = x_ref[...]
    ms = jnp.mean(x * x, axis=-1, keepdims=True)
    o_ref[...] = x * lax.rsqrt(ms + EPS) * g_ref[...]


def _final_norm(x, g):
    b, s, d = x.shape
    ts = _pick(s, 256)
    return pl.pallas_call(
        _final_norm_kernel,
        grid=(b, s // ts),
        in_specs=[pl.BlockSpec((None, ts, d), lambda i, j: (i, j, 0)),
                  pl.BlockSpec((1, d), lambda i, j: (0, 0))],
        out_specs=pl.BlockSpec((None, ts, d), lambda i, j: (i, j, 0)),
        out_shape=jax.ShapeDtypeStruct((b, s, d), F32),
        compiler_params=_params("arbitrary", "arbitrary"),
        name="final_norm",
    )(x, g)


def kernel(x, c, w_ada, b_ada, g_norm, w_in, g_sb, g_dil, w_out, g_final):
    b, s, d = x.shape
    depth = w_ada.shape[0]
    n_sb = g_sb.shape[1] // HEAD_DIM
    n_dil = g_dil.shape[1] // HEAD_DIM
    sb_w = n_sb * HEAD_DIM
    dil_w = n_dil * HEAD_DIM
    assert s % DIL_PAIRS[-1][0] == 0 and sb_w == dil_w

    qscale = LOG2E / math.sqrt(HEAD_DIM)
    colscale = jnp.concatenate([
        jnp.full((sb_w,), qscale, F32), jnp.ones((3 * sb_w,), F32),
        jnp.full((dil_w,), qscale, F32), jnp.ones((3 * dil_w,), F32)])[None, :]
    slopes = jnp.exp2(-ALIBI_MAX_BIAS * jnp.arange(1, n_dil + 1, dtype=F32) / n_dil) * LOG2E
    slopes = jnp.broadcast_to(slopes[:, None, None], (n_dil, 1, HEAD_DIM))

    pad = 16
    c_pad = jnp.zeros((pad, d), F32).at[:b].set(c.astype(F32))
    for layer in range(depth):
        mod = _ada(c_pad, w_ada[layer], b_ada[layer][None, :])[:b]
        shift, scale, gate = (mod[:, i * d:(i + 1) * d][:, None, :] for i in range(3))
        h = _norm_mod(x, g_norm[layer][None, :], scale, shift)
        proj = _inproj(h.reshape(b * s, d), w_in[layer], colscale).reshape(b, s, -1)
        y_sb = _sb_attention(proj, g_sb[layer][None, :], n_sb, 0)
        y_dl = _dil_attention(proj, g_dil[layer][None, :], slopes, n_dil, 4 * n_sb)
        x = _outproj(y_sb, y_dl, w_out[layer], x, gate)
    return _final_norm(x, g_final[None, :])
```

```python
import functools
import math

import jax
import jax.numpy as jnp
from jax import lax
from jax.experimental import pallas as pl
from jax.experimental.pallas import tpu as pltpu

HEAD_DIM = 128
EPS = 1e-6
LOG2E = 1.4426950408889634
ALIBI_MAX_BIAS = 8.0
DIL_PAIRS = ((128, 1), (512, 4), (2048, 16))
DIL_BLOCK = 128
DIL_PERM = 4
MASK_BIAS = -1e30
V7X_VMEM_BYTES = 64 * 1024 * 1024
VMEM_LIMIT = V7X_VMEM_BYTES - 8 * 1024 * 1024

F32 = jnp.float32
BF16 = jnp.bfloat16


def _params(*sem):
    return pltpu.CompilerParams(dimension_semantics=sem, vmem_limit_bytes=VMEM_LIMIT)


def _pick(n, pref):
    t = min(pref, n)
    while n % t:
        t //= 2
    return t


def _silu(x):
    return x / (1.0 + jnp.exp(-x))


def _dot_nt(a, b):
    return lax.dot_general(a, b, (((1,), (1,)), ((), ())), preferred_element_type=F32)


def _ada_kernel(c_ref, w_ref, b_ref, o_ref):
    cs = _silu(c_ref[...])
    acc = jnp.dot(cs.astype(BF16), w_ref[...].astype(BF16), preferred_element_type=F32)
    o_ref[...] = acc + b_ref[...]


def _ada(c_pad, w, b):
    m, d = c_pad.shape
    n = w.shape[1]
    tn = _pick(n, 512)
    return pl.pallas_call(
        _ada_kernel,
        grid=(n // tn,),
        in_specs=[
            pl.BlockSpec((m, d), lambda j: (0, 0)),
            pl.BlockSpec((d, tn), lambda j: (0, j)),
            pl.BlockSpec((1, tn), lambda j: (0, j)),
        ],
        out_specs=pl.BlockSpec((m, tn), lambda j: (0, j)),
        out_shape=jax.ShapeDtypeStruct((m, n), F32),
        compiler_params=_params("arbitrary"),
        name="ada_ln",
    )(c_pad, w, b)


def _norm_mod_kernel(x_ref, g_ref, scale_ref, shift_ref, h_ref):
    x = x_ref[...]
    ms = jnp.mean(x * x, axis=-1, keepdims=True)
    y = x * lax.rsqrt(ms + EPS) * g_ref[...]
    h_ref[...] = (y * (1.0 + scale_ref[...]) + shift_ref[...]).astype(BF16)


def _norm_mod(x, g, scale, shift):
    b, s, d = x.shape
    ts = _pick(s, 256)
    return pl.pallas_call(
        _norm_mod_kernel,
        grid=(b, s // ts),
        in_specs=[
            pl.BlockSpec((None, ts, d), lambda i, j: (i, j, 0)),
            pl.BlockSpec((1, d), lambda i, j: (0, 0)),
            pl.BlockSpec((None, 1, d), lambda i, j: (i, 0, 0)),
            pl.BlockSpec((None, 1, d), lambda i, j: (i, 0, 0)),
        ],
        out_specs=pl.BlockSpec((None, ts, d), lambda i, j: (i, j, 0)),
        out_shape=jax.ShapeDtypeStruct((b, s, d), BF16),
        compiler_params=_params("arbitrary", "arbitrary"),
        name="norm_mod",
    )(x, g, scale, shift)


def _inproj_kernel(h_ref, w_ref, cs_ref, o_ref, wb_ref):
    @pl.when(pl.program_id(1) == 0)
    def _():
        wb_ref[...] = w_ref[...].astype(BF16)

    acc = jnp.dot(h_ref[...], wb_ref[...], preferred_element_type=F32)
    o_ref[...] = (acc * cs_ref[...]).astype(BF16)


def _inproj(h, w, colscale, col0, ncols, perm):
    b, s, d = h.shape
    seg = s // perm
    tm = _pick(seg, 512)
    tn = _pick(ncols, 512)
    nl = seg // tm
    joff = col0 // tn
    assert col0 % tn == 0
    h_view = h.reshape(b, seg, perm * d)

    def row(i):
        return i // (perm * nl), (i // nl) % perm, i % nl

    out = pl.pallas_call(
        _inproj_kernel,
        grid=(ncols // tn, b * perm * nl),
        in_specs=[
            pl.BlockSpec((None, tm, d), lambda j, i: (row(i)[0], row(i)[2], row(i)[1])),
            pl.BlockSpec((d, tn), lambda j, i: (0, joff + j)),
            pl.BlockSpec((1, tn), lambda j, i: (0, joff + j)),
        ],
        out_specs=pl.BlockSpec((None, None, tm, tn), lambda j, i: (*row(i), j)),
        out_shape=jax.ShapeDtypeStruct((b, perm, seg, ncols), BF16),
        scratch_shapes=[pltpu.VMEM((d, tn), BF16)],
        compiler_params=_params("arbitrary", "arbitrary"),
        name="in_proj_p%d" % perm,
    )(h_view, w, colscale)
    return out.reshape(b, s, ncols)


def _head_epilogue(o, g, zg):
    ms = jnp.mean(o * o, axis=-1, keepdims=True)
    return o * lax.rsqrt(ms + EPS) * g * _silu(zg)


def _sb_kernel(q_ref, k_ref, v_ref, z_ref, g_ref, o_ref, acc_ref, carry_ref, *, seq, tile):
    row = lax.broadcasted_iota(jnp.int32, (tile, tile), 0)
    col = lax.broadcasted_iota(jnp.int32, (tile, tile), 1)
    causal = col < row
    upper = jnp.where(row > col, 1.0, 0.0).astype(BF16)

    def key_tile(q, k, v, carry, diag):
        s = _dot_nt(q, k)
        e = jnp.exp2(-jnp.abs(s))
        lb = jnp.minimum(s, 0.0) - jnp.log(1.0 + e) * LOG2E
        lom = lb - s
        if diag:
            lom = jnp.where(causal, lom, 0.0)
        arg = lb + jnp.dot(lom.astype(BF16), upper, preferred_element_type=F32)
        if carry is not None:
            arg = arg + carry
        a = jnp.exp2(arg)
        if diag:
            a = jnp.where(causal, a, 0.0)
        av = jnp.dot(a.astype(BF16), v, preferred_element_type=F32)
        return av, jnp.sum(lom, axis=1, keepdims=True)

    for kj in reversed(range(seq // tile)):
        k0 = kj * tile
        k = k_ref[pl.ds(k0, tile), :]
        v = v_ref[pl.ds(k0, tile), :]
        av, rsum = key_tile(q_ref[pl.ds(k0, tile), :], k, v, None, True)
        acc_ref[pl.ds(k0, tile), :] = av
        carry_ref[pl.ds(k0, tile), :] = rsum
        if k0 + tile < seq:
            later = pl.ds(k0 + tile, seq - k0 - tile)
            av, rsum = key_tile(q_ref[later, :], k, v, carry_ref[later, :], False)
            acc_ref[later, :] += av
            carry_ref[later, :] += rsum

    for q0 in range(0, seq, tile):
        rows = pl.ds(q0, tile)
        zg = z_ref[rows, :].astype(F32)
        o_ref[rows, :] = _head_epilogue(acc_ref[rows, :], g_ref[...], zg).astype(BF16)


def _head_spec(s, n_heads, off):
    return pl.BlockSpec((None, s, HEAD_DIM), lambda i, h: (i, 0, off * n_heads + h))


def _sb_attention(proj, g_sb, n_heads):
    b, s, _ = proj.shape
    tile = _pick(s, 256)
    return pl.pallas_call(
        functools.partial(_sb_kernel, seq=s, tile=tile),
        grid=(b, n_heads),
        in_specs=[_head_spec(s, n_heads, 0), _head_spec(s, n_heads, 1), _head_spec(s, n_heads, 2),
                  _head_spec(s, n_heads, 3), pl.BlockSpec((1, HEAD_DIM), lambda i, h: (0, h))],
        out_specs=pl.BlockSpec((None, s, HEAD_DIM), lambda i, h: (i, 0, h)),
        out_shape=jax.ShapeDtypeStruct((b, s, n_heads * HEAD_DIM), BF16),
        scratch_shapes=[pltpu.VMEM((s, HEAD_DIM), F32), pltpu.VMEM((s, 1), F32)],
        compiler_params=_params("arbitrary", "arbitrary"),
        name="sb_attn",
    )(proj, proj, proj, proj, g_sb)


def _dil_kernel(q_ref, k_ref, v_ref, z_ref, g_ref, slope_ref, o_ref,
                qf, kf, vf, q16, k16, vo, vo16, num_s, den_s, max_s, onat, *, seq):
    n = DIL_BLOCK
    seg = seq // DIL_PERM
    sub = n // DIL_PERM
    slope = slope_ref[0:1, 0:1]

    def bias_of(dist, window):
        return jnp.where((dist >= 0) & (dist <= window), dist.astype(F32) * (-slope), MASK_BIAS)

    def iota2(nk):
        return (lax.broadcasted_iota(jnp.int32, (n, nk), 0), lax.broadcasted_iota(jnp.int32, (n, nk), 1))

    def attend(q, k, vv, bias):
        s = _dot_nt(q, k) + bias
        m = jnp.max(s, axis=1, keepdims=True)
        p = jnp.exp2(s - m)
        pv = jnp.dot(p.astype(BF16), vv, preferred_element_type=F32)
        return pv[:, :HEAD_DIM], pv[:, HEAD_DIM:], jnp.broadcast_to(m, (n, HEAD_DIM))

    def put(br, idx, res):
        num_s[br, idx, :], den_s[br, idx, :], max_s[br, idx, :] = res

    ones = jnp.ones((seq, HEAD_DIM), BF16)
    vo[:, :HEAD_DIM] = v_ref[...]
    vo[:, HEAD_DIM:] = ones
    vo16[:, HEAD_DIM:] = ones

    a, c = iota2(2 * n)
    tq = (a % sub) * DIL_PERM + a // sub
    bias_prev = bias_of(tq - ((c % (2 * sub)) * DIL_PERM + c // (2 * sub) - n), DIL_PAIRS[0][0])
    a, c = iota2(n)
    bias_first = bias_of(tq[:, :n] - ((c % sub) * DIL_PERM + c // sub), DIL_PAIRS[0][0])

    def gather(ref, start, size):
        return jnp.concatenate([ref[pl.ds(res * seg + start, size), :] for res in range(DIL_PERM)], axis=0)

    for i in range(seq // n):
        q = gather(q_ref, sub * i, sub)
        if i == 0:
            res3 = attend(q, gather(k_ref, 0, sub), gather(vo, 0, sub), bias_first)
        else:
            res3 = attend(q, gather(k_ref, sub * (i - 1), 2 * sub), gather(vo, sub * (i - 1), 2 * sub), bias_prev)
        for res in range(DIL_PERM):
            put(0, pl.ds(res * seg + sub * i, sub), tuple(t[res * sub:(res + 1) * sub] for t in res3))

    r = DIL_PAIRS[1][1]
    assert r == DIL_PERM
    a, c = iota2(2 * n)
    bias_prev = bias_of(r * (a - c + n), DIL_PAIRS[1][0])
    bias_first = bias_prev[:, n:]
    for res in range(DIL_PERM):
        for i in range(seg // n):
            row0 = res * seg + i * n
            q = q_ref[pl.ds(row0, n), :]
            if i == 0:
                res3 = attend(q, k_ref[pl.ds(row0, n), :], vo[pl.ds(row0, n), :], bias_first)
            else:
                res3 = attend(q, k_ref[pl.ds(row0 - n, 2 * n), :], vo[pl.ds(row0 - n, 2 * n), :], bias_prev)
            put(1, pl.ds(row0, n), res3)

    r = DIL_PAIRS[2][1]
    step = r // DIL_PERM
    assert seq // r == n
    a, c = iota2(n)
    bias_first = bias_of(r * (a - c), DIL_PAIRS[2][0])
    qf[...] = q_ref[...].astype(F32)
    kf[...] = k_ref[...].astype(F32)
    vf[...] = v_ref[...].astype(F32)
    for res in range(DIL_PERM):
        for sres in range(step):
            src = pl.ds(res * seg + sres, n, stride=step)
            dst = pl.ds((res * step + sres) * n, n)
            q16[dst, :] = qf[src, :].astype(BF16)
            k16[dst, :] = kf[src, :].astype(BF16)
            vo16[dst, :HEAD_DIM] = vf[src, :].astype(BF16)
    for res in range(DIL_PERM):
        for sres in range(step):
            blk = pl.ds((res * step + sres) * n, n)
            res3 = attend(q16[blk, :], k16[blk, :], vo16[blk, :], bias_first)
            put(2, pl.ds(res * seg + sres, n, stride=step), res3)

    chunk = _pick(seg, 256)
    for c0 in range(0, seq, chunk):
        rows = pl.ds(c0, chunk)
        m_all = jnp.maximum(jnp.maximum(max_s[0, rows, :], max_s[1, rows, :]), max_s[2, rows, :])
        num = jnp.zeros((chunk, HEAD_DIM), F32)
        den = jnp.zeros((chunk, HEAD_DIM), F32)
        for br in range(len(DIL_PAIRS)):
            w = jnp.exp2(max_s[br, rows, :] - m_all)
            num = num + num_s[br, rows, :] * w
            den = den + den_s[br, rows, :] * w
        zg = z_ref[rows, :].astype(F32)
        res, l0 = c0 // seg, c0 % seg
        onat[pl.ds(DIL_PERM * l0 + res, chunk, stride=DIL_PERM), :] = _head_epilogue(num / den, g_ref[...], zg)
    for c0 in range(0, seq, chunk):
        o_ref[pl.ds(c0, chunk), :] = onat[pl.ds(c0, chunk), :].astype(BF16)


def _dil_attention(proj, g_dil, slopes, n_heads):
    b, s, _ = proj.shape
    nbr = len(DIL_PAIRS)
    rows_f32 = pltpu.VMEM((s, HEAD_DIM), F32)
    rows_bf16 = pltpu.VMEM((s, HEAD_DIM), BF16)
    rows2_bf16 = pltpu.VMEM((s, 2 * HEAD_DIM), BF16)
    per_branch = pltpu.VMEM((nbr, s, HEAD_DIM), F32)
    return pl.pallas_call(
        functools.partial(_dil_kernel, seq=s),
        grid=(b, n_heads),
        in_specs=[_head_spec(s, n_heads, 0), _head_spec(s, n_heads, 1), _head_spec(s, n_heads, 2),
                  _head_spec(s, n_heads, 3), pl.BlockSpec((1, HEAD_DIM), lambda i, h: (0, h)),
                  pl.BlockSpec((None, 1, HEAD_DIM), lambda i, h: (h, 0, 0))],
        out_specs=pl.BlockSpec((None, s, HEAD_DIM), lambda i, h: (i, 0, h)),
        out_shape=jax.ShapeDtypeStruct((b, s, n_heads * HEAD_DIM), BF16),
        scratch_shapes=[rows_f32, rows_f32, rows_f32, rows_bf16, rows_bf16, rows2_bf16, rows2_bf16,
                        per_branch, per_branch, per_branch, rows_f32],
        compiler_params=_params("arbitrary", "arbitrary"),
        name="dil_attn",
    )(proj, proj, proj, proj, g_dil, slopes)


def _outproj_kernel(ya_ref, yb_ref, wa_ref, wb_ref, x_ref, gate_ref, o_ref, wab_ref, wbb_ref):
    @pl.when(pl.program_id(1) == 0)
    def _():
        wab_ref[...] = wa_ref[...].astype(BF16)
        wbb_ref[...] = wb_ref[...].astype(BF16)

    acc = jnp.dot(ya_ref[...], wab_ref[...], preferred_element_type=F32)
    acc = acc + jnp.dot(yb_ref[...], wbb_ref[...], preferred_element_type=F32)
    o_ref[...] = x_ref[...] + gate_ref[...] * acc


def _outproj(y_sb, y_dl, w_out, x, gate):
    b, s, d = x.shape
    ka = y_sb.shape[2]
    kb = y_dl.shape[2]
    tm = _pick(s, 512)
    tn = _pick(d, 512)
    nm = s // tm
    return pl.pallas_call(
        _outproj_kernel,
        grid=(d // tn, b * nm),
        in_specs=[
            pl.BlockSpec((None, tm, ka), lambda j, i: (i // nm, i % nm, 0)),
            pl.BlockSpec((None, tm, kb), lambda j, i: (i // nm, i % nm, 0)),
            pl.BlockSpec((ka, tn), lambda j, i: (0, j)),
            pl.BlockSpec((kb, tn), lambda j, i: (ka // kb, j)),
            pl.BlockSpec((None, tm, tn), lambda j, i: (i // nm, i % nm, j)),
            pl.BlockSpec((None, 1, tn), lambda j, i: (i // nm, 0, j)),
        ],
        out_specs=pl.BlockSpec((None, tm, tn), lambda j, i: (i // nm, i % nm, j)),
        out_shape=jax.ShapeDtypeStruct((b, s, d), F32),
        scratch_shapes=[pltpu.VMEM((ka, tn), BF16), pltpu.VMEM((kb, tn), BF16)],
        compiler_params=_params("arbitrary", "arbitrary"),
        name="out_proj",
    )(y_sb, y_dl, w_out, w_out, x, gate)


def _final_norm_kernel(x_ref, g_ref, o_ref):
    x = x_ref[...]
    ms = jnp.mean(x * x, axis=-1, keepdims=True)
    o_ref[...] = x * lax.rsqrt(ms + EPS) * g_ref[...]


def _final_norm(x, g):
    b, s, d = x.shape
    ts = _pick(s, 256)
    return pl.pallas_call(
        _final_norm_kernel,
        grid=(b, s // ts),
        in_specs=[pl.BlockSpec((None, ts, d), lambda i, j: (i, j, 0)),
                  pl.BlockSpec((1, d), lambda i, j: (0, 0))],
        out_specs=pl.BlockSpec((None, ts, d), lambda i, j: (i, j, 0)),
        out_shape=jax.ShapeDtypeStruct((b, s, d), F32),
        compiler_params=_params("arbitrary", "arbitrary"),
        name="final_norm",
    )(x, g)


def kernel(x, c, w_ada, b_ada, g_norm, w_in, g_sb, g_dil, w_out, g_final):
    b, s, d = x.shape
    depth = w_ada.shape[0]
    n_sb = g_sb.shape[1] // HEAD_DIM
    n_dil = g_dil.shape[1] // HEAD_DIM
    sb_w = n_sb * HEAD_DIM
    dil_w = n_dil * HEAD_DIM
    assert s % DIL_PAIRS[-1][0] == 0 and sb_w == dil_w

    qscale = LOG2E / math.sqrt(HEAD_DIM)
    colscale = jnp.concatenate([
        jnp.full((sb_w,), qscale, F32), jnp.ones((3 * sb_w,), F32),
        jnp.full((dil_w,), qscale, F32), jnp.ones((3 * dil_w,), F32)])[None, :]
    slopes = jnp.exp2(-ALIBI_MAX_BIAS * jnp.arange(1, n_dil + 1, dtype=F32) / n_dil) * LOG2E
    slopes = jnp.broadcast_to(slopes[:, None, None], (n_dil, 1, HEAD_DIM))

    pad = 16
    c_pad = jnp.zeros((pad, d), F32).at[:b].set(c.astype(F32))
    for layer in range(depth):
        mod = _ada(c_pad, w_ada[layer], b_ada[layer][None, :])[:b]
        shift, scale, gate = (mod[:, i * d:(i + 1) * d][:, None, :] for i in range(3))
        h = _norm_mod(x, g_norm[layer][None, :], scale, shift)
        proj_sb = _inproj(h, w_in[layer], colscale, 0, 4 * sb_w, 1)
        proj_dl = _inproj(h, w_in[layer], colscale, 4 * sb_w, 4 * dil_w, DIL_PERM)
        y_sb = _sb_attention(proj_sb, g_sb[layer][None, :], n_sb)
        y_dl = _dil_attention(proj_dl, g_dil[layer][None, :], slopes, n_dil)
        x = _outproj(y_sb, y_dl, w_out[layer], x, gate)
    return _final_norm(x, g_final[None, :])
```

```python
import functools
import math

import jax
import jax.numpy as jnp
from jax import lax
from jax.experimental import pallas as pl
from jax.experimental.pallas import tpu as pltpu

HEAD_DIM = 128
EPS = 1e-6
LOG2E = 1.4426950408889634
ALIBI_MAX_BIAS = 8.0
DIL_PAIRS = ((128, 1), (512, 4), (2048, 16))
DIL_BLOCK = 128
DIL_PERM = 4
MASK_BIAS = -1e30
V7X_VMEM_BYTES = 64 * 1024 * 1024
VMEM_LIMIT = V7X_VMEM_BYTES - 8 * 1024 * 1024
IN_PROJ_TM, IN_PROJ_TN = 1024, 512
OUT_PROJ_TM, OUT_PROJ_TN = 1024, 512

F32 = jnp.float32
BF16 = jnp.bfloat16


def _params(*sem):
    return pltpu.CompilerParams(dimension_semantics=sem, vmem_limit_bytes=VMEM_LIMIT)


def _pick(n, pref):
    t = min(pref, n)
    while n % t:
        t //= 2
    return t


def _silu(x):
    return x / (1.0 + jnp.exp(-x))


def _dot_nt(a, b):
    return lax.dot_general(a, b, (((1,), (1,)), ((), ())), preferred_element_type=F32)


def _ada_kernel(c_ref, w_ref, b_ref, o_ref):
    cs = _silu(c_ref[...])
    acc = jnp.dot(cs.astype(BF16), w_ref[...].astype(BF16), preferred_element_type=F32)
    o_ref[...] = acc + b_ref[...]


def _ada(c_pad, w, b):
    m, d = c_pad.shape
    n = w.shape[1]
    tn = _pick(n, 512)
    return pl.pallas_call(
        _ada_kernel,
        grid=(n // tn,),
        in_specs=[
            pl.BlockSpec((m, d), lambda j: (0, 0)),
            pl.BlockSpec((d, tn), lambda j: (0, j)),
            pl.BlockSpec((1, tn), lambda j: (0, j)),
        ],
        out_specs=pl.BlockSpec((m, tn), lambda j: (0, j)),
        out_shape=jax.ShapeDtypeStruct((m, n), F32),
        compiler_params=_params("arbitrary"),
        name="ada_ln",
    )(c_pad, w, b)


def _norm_mod_kernel(x_ref, g_ref, scale_ref, shift_ref, h_ref):
    x = x_ref[...]
    ms = jnp.mean(x * x, axis=-1, keepdims=True)
    y = x * lax.rsqrt(ms + EPS) * g_ref[...]
    h_ref[...] = (y * (1.0 + scale_ref[...]) + shift_ref[...]).astype(BF16)


def _norm_mod(x, g, scale, shift):
    b, s, d = x.shape
    ts = _pick(s, 256)
    return pl.pallas_call(
        _norm_mod_kernel,
        grid=(b, s // ts),
        in_specs=[
            pl.BlockSpec((None, ts, d), lambda i, j: (i, j, 0)),
            pl.BlockSpec((1, d), lambda i, j: (0, 0)),
            pl.BlockSpec((None, 1, d), lambda i, j: (i, 0, 0)),
            pl.BlockSpec((None, 1, d), lambda i, j: (i, 0, 0)),
        ],
        out_specs=pl.BlockSpec((None, ts, d), lambda i, j: (i, j, 0)),
        out_shape=jax.ShapeDtypeStruct((b, s, d), BF16),
        compiler_params=_params("arbitrary", "arbitrary"),
        name="norm_mod",
    )(x, g, scale, shift)


def _inproj_tile(h_ref, w_ref, cs_ref, wb_ref, emit, gate_from):
    @pl.when(pl.program_id(1) == 0)
    def _():
        wb_ref[...] = w_ref[...].astype(BF16)

    acc = jnp.dot(h_ref[...], wb_ref[...], preferred_element_type=F32)
    is_gate = pl.program_id(0) >= gate_from

    @pl.when(is_gate)
    def _():
        emit(_silu(acc))

    @pl.when(jnp.logical_not(is_gate))
    def _():
        emit(acc * cs_ref[...])


def _inproj_kernel(h_ref, w_ref, cs_ref, o_ref, wb_ref, *, gate_from):
    def emit(vals):
        o_ref[...] = vals.astype(BF16)

    _inproj_tile(h_ref, w_ref, cs_ref, wb_ref, emit, gate_from)


def _inproj_perm_kernel(h_ref, w_ref, cs_ref, o_ref, wb_ref, acc_ref, *, gate_from, perm):
    def emit(vals):
        tm, tn = vals.shape
        lanes = acc_ref.shape[2]
        for slab in range(tn // lanes):
            acc_ref[slab] = vals[:, slab * lanes:(slab + 1) * lanes]
        for res in range(perm):
            for slab in range(tn // lanes):
                rows = acc_ref[slab, pl.ds(res, tm // perm, stride=perm), :]
                o_ref[res, :, slab * lanes:(slab + 1) * lanes] = rows.astype(BF16)

    _inproj_tile(h_ref, w_ref, cs_ref, wb_ref, emit, gate_from)


def _inproj(h, w, colscale, col0, ncols, perm):
    b, s, d = h.shape
    tm = _pick(s, IN_PROJ_TM)
    tn = _pick(ncols // 4, IN_PROJ_TN)
    nm = s // tm
    joff = col0 // tn
    gate_from = (3 * ncols // 4) // tn
    assert col0 % tn == 0 and tm % (8 * perm) == 0
    in_specs = [
        pl.BlockSpec((None, tm, d), lambda j, i: (i // nm, i % nm, 0)),
        pl.BlockSpec((d, tn), lambda j, i: (0, joff + j)),
        pl.BlockSpec((1, tn), lambda j, i: (0, joff + j)),
    ]
    scratch = [pltpu.VMEM((d, tn), BF16)]
    if perm == 1:
        body = functools.partial(_inproj_kernel, gate_from=gate_from)
        out_spec = pl.BlockSpec((None, tm, tn), lambda j, i: (i // nm, i % nm, j))
        out_shape = (b, s, ncols)
    else:
        body = functools.partial(_inproj_perm_kernel, gate_from=gate_from, perm=perm)
        out_spec = pl.BlockSpec((None, perm, tm // perm, tn), lambda j, i: (i // nm, 0, i % nm, j))
        out_shape = (b, perm, s // perm, ncols)
        scratch.append(pltpu.VMEM((tn // HEAD_DIM, tm, HEAD_DIM), F32))
    out = pl.pallas_call(
        body,
        grid=(ncols // tn, b * nm),
        in_specs=in_specs,
        out_specs=out_spec,
        out_shape=jax.ShapeDtypeStruct(out_shape, BF16),
        scratch_shapes=scratch,
        compiler_params=_params("arbitrary", "arbitrary"),
        name="in_proj_p%d" % perm,
    )(h, w, colscale)
    return out.reshape(b, s, ncols)


def _head_epilogue(o, g, sg):
    ms = jnp.mean(o * o, axis=-1, keepdims=True)
    return o * lax.rsqrt(ms + EPS) * g * sg.astype(F32)


def _sb_kernel(q_ref, k_ref, v_ref, z_ref, g_ref, o_ref, acc_ref, carry_ref, *, seq, tile):
    row = lax.broadcasted_iota(jnp.int32, (tile, tile), 0)
    col = lax.broadcasted_iota(jnp.int32, (tile, tile), 1)
    causal = col < row
    upper = jnp.where(row > col, 1.0, 0.0).astype(BF16)

    def key_tile(q, k, v, carry, mask):
        s = _dot_nt(q, k)
        e = jnp.exp2(-jnp.abs(s))
        lb = jnp.minimum(s, 0.0) - jnp.log(1.0 + e) * LOG2E
        lom = lb - s
        if mask is not None:
            lom = jnp.where(mask, lom, 0.0)
        arg = lb + jnp.dot(lom.astype(BF16), upper, preferred_element_type=F32)
        if carry is not None:
            arg = arg + carry
        a = jnp.exp2(arg)
        if mask is not None:
            a = jnp.where(mask, a, 0.0)
        av = jnp.dot(a.astype(BF16), v, preferred_element_type=F32)
        return av, jnp.sum(lom, axis=1, keepdims=True)

    for kj in reversed(range(seq // tile)):
        k0 = kj * tile
        k = k_ref[pl.ds(k0, tile), :]
        v = v_ref[pl.ds(k0, tile), :]
        av, rsum = key_tile(q_ref[pl.ds(k0, tile), :], k, v, None, causal)
        acc_ref[pl.ds(k0, tile), :] = av
        carry_ref[pl.ds(k0, tile), :] = rsum
        if k0 + tile < seq:
            later = pl.ds(k0 + tile, seq - k0 - tile)
            av, rsum = key_tile(q_ref[later, :], k, v, carry_ref[later, :], None)
            acc_ref[later, :] += av
            carry_ref[later, :] += rsum

    for q0 in range(0, seq, tile):
        rows = pl.ds(q0, tile)
        o_ref[rows, :] = _head_epilogue(acc_ref[rows, :], g_ref[...], z_ref[rows, :]).astype(BF16)


def _head_spec(s, n_heads, off):
    return pl.BlockSpec((None, s, HEAD_DIM), lambda i, h: (i, 0, off * n_heads + h))


def _sb_attention(proj, g_sb, n_heads):
    b, s, _ = proj.shape
    tile = _pick(s, 256)
    return pl.pallas_call(
        functools.partial(_sb_kernel, seq=s, tile=tile),
        grid=(b, n_heads),
        in_specs=[_head_spec(s, n_heads, 0), _head_spec(s, n_heads, 1), _head_spec(s, n_heads, 2),
                  _head_spec(s, n_heads, 3), pl.BlockSpec((1, HEAD_DIM), lambda i, h: (0, h))],
        out_specs=pl.BlockSpec((None, s, HEAD_DIM), lambda i, h: (i, 0, h)),
        out_shape=jax.ShapeDtypeStruct((b, s, n_heads * HEAD_DIM), BF16),
        scratch_shapes=[pltpu.VMEM((s, HEAD_DIM), F32), pltpu.VMEM((s, 1), F32)],
        compiler_params=_params("arbitrary", "arbitrary"),
        name="sb_attn",
    )(proj, proj, proj, proj, g_sb)


def _dil_kernel(q_ref, k_ref, v_ref, z_ref, g_ref, slope_ref, o_ref,
                qf, kf, vf, q16, k16, vo, vo16, num_s, den_s, max_s, onat, *, seq):
    n = DIL_BLOCK
    seg = seq // DIL_PERM
    sub = n // DIL_PERM
    slope = slope_ref[0:1, 0:1]

    def bias_of(dist, window):
        return jnp.where((dist >= 0) & (dist <= window), dist.astype(F32) * (-slope), MASK_BIAS)

    def iota2(nk):
        return (lax.broadcasted_iota(jnp.int32, (n, nk), 0), lax.broadcasted_iota(jnp.int32, (n, nk), 1))

    def attend(q, k, vv, bias):
        s = _dot_nt(q, k) + bias
        m = jnp.max(s, axis=1, keepdims=True)
        p = jnp.exp2(s - m)
        pv = jnp.dot(p.astype(BF16), vv, preferred_element_type=F32)
        return pv[:, :HEAD_DIM], pv[:, HEAD_DIM:], jnp.broadcast_to(m, (n, HEAD_DIM))

    def put(br, idx, res):
        num_s[br, idx, :], den_s[br, idx, :], max_s[br, idx, :] = res

    ones = jnp.ones((seq, HEAD_DIM), BF16)
    vo[:, :HEAD_DIM] = v_ref[...]
    vo[:, HEAD_DIM:] = ones
    vo16[:, HEAD_DIM:] = ones

    a, c = iota2(2 * n)
    tq = (a % sub) * DIL_PERM + a // sub
    bias_prev = bias_of(tq - ((c % (2 * sub)) * DIL_PERM + c // (2 * sub) - n), DIL_PAIRS[0][0])
    a, c = iota2(n)
    bias_first = bias_of(tq[:, :n] - ((c % sub) * DIL_PERM + c // sub), DIL_PAIRS[0][0])

    def gather(ref, start, size):
        return jnp.concatenate([ref[pl.ds(res * seg + start, size), :] for res in range(DIL_PERM)], axis=0)

    for i in range(seq // n):
        q = gather(q_ref, sub * i, sub)
        if i == 0:
            res3 = attend(q, gather(k_ref, 0, sub), gather(vo, 0, sub), bias_first)
        else:
            res3 = attend(q, gather(k_ref, sub * (i - 1), 2 * sub), gather(vo, sub * (i - 1), 2 * sub), bias_prev)
        for res in range(DIL_PERM):
            put(0, pl.ds(res * seg + sub * i, sub), tuple(t[res * sub:(res + 1) * sub] for t in res3))

    r = DIL_PAIRS[1][1]
    assert r == DIL_PERM
    a, c = iota2(2 * n)
    bias_prev = bias_of(r * (a - c + n), DIL_PAIRS[1][0])
    bias_first = bias_prev[:, n:]
    r16 = DIL_PAIRS[2][1]
    step = r16 // DIL_PERM
    assert seq // r16 == n
    a, c = iota2(n)
    bias_16 = bias_of(r16 * (a - c), DIL_PAIRS[2][0])
    chunk = _pick(seg, 256)

    for res in range(DIL_PERM):
        grp = pl.ds(res * seg, seg)
        for i in range(seg // n):
            row0 = res * seg + i * n
            q = q_ref[pl.ds(row0, n), :]
            if i == 0:
                res3 = attend(q, k_ref[pl.ds(row0, n), :], vo[pl.ds(row0, n), :], bias_first)
            else:
                res3 = attend(q, k_ref[pl.ds(row0 - n, 2 * n), :], vo[pl.ds(row0 - n, 2 * n), :], bias_prev)
            put(1, pl.ds(row0, n), res3)

        qf[grp, :] = q_ref[grp, :].astype(F32)
        kf[grp, :] = k_ref[grp, :].astype(F32)
        vf[grp, :] = v_ref[grp, :].astype(F32)
        for sres in range(step):
            src = pl.ds(res * seg + sres, n, stride=step)
            blk = pl.ds((res * step + sres) * n, n)
            q16[blk, :] = qf[src, :].astype(BF16)
            k16[blk, :] = kf[src, :].astype(BF16)
            vo16[blk, :HEAD_DIM] = vf[src, :].astype(BF16)
            put(2, src, attend(q16[blk, :], k16[blk, :], vo16[blk, :], bias_16))

        for l0 in range(0, seg, chunk):
            rows = pl.ds(res * seg + l0, chunk)
            m_all = jnp.maximum(jnp.maximum(max_s[0, rows, :], max_s[1, rows, :]), max_s[2, rows, :])
            num = jnp.zeros((chunk, HEAD_DIM), F32)
            den = jnp.zeros((chunk, HEAD_DIM), F32)
            for br in range(len(DIL_PAIRS)):
                w = jnp.exp2(max_s[br, rows, :] - m_all)
                num = num + num_s[br, rows, :] * w
                den = den + den_s[br, rows, :] * w
            onat[pl.ds(DIL_PERM * l0 + res, chunk, stride=DIL_PERM), :] = _head_epilogue(
                num / den, g_ref[...], z_ref[rows, :])
    for c0 in range(0, seq, chunk):
        o_ref[pl.ds(c0, chunk), :] = onat[pl.ds(c0, chunk), :].astype(BF16)


def _dil_attention(proj, g_dil, slopes, n_heads):
    b, s, _ = proj.shape
    nbr = len(DIL_PAIRS)
    rows_f32 = pltpu.VMEM((s, HEAD_DIM), F32)
    rows_bf16 = pltpu.VMEM((s, HEAD_DIM), BF16)
    rows2_bf16 = pltpu.VMEM((s, 2 * HEAD_DIM), BF16)
    per_branch = pltpu.VMEM((nbr, s, HEAD_DIM), F32)
    return pl.pallas_call(
        functools.partial(_dil_kernel, seq=s),
        grid=(b, n_heads),
        in_specs=[_head_spec(s, n_heads, 0), _head_spec(s, n_heads, 1), _head_spec(s, n_heads, 2),
                  _head_spec(s, n_heads, 3), pl.BlockSpec((1, HEAD_DIM), lambda i, h: (0, h)),
                  pl.BlockSpec((None, 1, HEAD_DIM), lambda i, h: (h, 0, 0))],
        out_specs=pl.BlockSpec((None, s, HEAD_DIM), lambda i, h: (i, 0, h)),
        out_shape=jax.ShapeDtypeStruct((b, s, n_heads * HEAD_DIM), BF16),
        scratch_shapes=[rows_f32, rows_f32, rows_f32, rows_bf16, rows_bf16, rows2_bf16, rows2_bf16,
                        per_branch, per_branch, per_branch, rows_f32],
        compiler_params=_params("arbitrary", "arbitrary"),
        name="dil_attn",
    )(proj, proj, proj, proj, g_dil, slopes)


def _outproj_kernel(ya_ref, yb_ref, wa_ref, wb_ref, x_ref, gate_ref, o_ref, wab_ref, wbb_ref):
    @pl.when(pl.program_id(1) == 0)
    def _():
        wab_ref[...] = wa_ref[...].astype(BF16)
        wbb_ref[...] = wb_ref[...].astype(BF16)

    acc = jnp.dot(ya_ref[...], wab_ref[...], preferred_element_type=F32)
    acc = acc + jnp.dot(yb_ref[...], wbb_ref[...], preferred_element_type=F32)
    o_ref[...] = x_ref[...] + gate_ref[...] * acc


def _outproj(y_sb, y_dl, w_out, x, gate):
    b, s, d = x.shape
    ka = y_sb.shape[2]
    kb = y_dl.shape[2]
    tm = _pick(s, OUT_PROJ_TM)
    tn = _pick(d, OUT_PROJ_TN)
    nm = s // tm
    return pl.pallas_call(
        _outproj_kernel,
        grid=(d // tn, b * nm),
        in_specs=[
            pl.BlockSpec((None, tm, ka), lambda j, i: (i // nm, i % nm, 0)),
            pl.BlockSpec((None, tm, kb), lambda j, i: (i // nm, i % nm, 0)),
            pl.BlockSpec((ka, tn), lambda j, i: (0, j)),
            pl.BlockSpec((kb, tn), lambda j, i: (ka // kb, j)),
            pl.BlockSpec((None, tm, tn), lambda j, i: (i // nm, i % nm, j)),
            pl.BlockSpec((None, 1, tn), lambda j, i: (i // nm, 0, j)),
        ],
        out_specs=pl.BlockSpec((None, tm, tn), lambda j, i: (i // nm, i % nm, j)),
        out_shape=jax.ShapeDtypeStruct((b, s, d), F32),
        scratch_shapes=[pltpu.VMEM((ka, tn), BF16), pltpu.VMEM((kb, tn), BF16)],
        compiler_params=_params("arbitrary", "arbitrary"),
        name="out_proj",
    )(y_sb, y_dl, w_out, w_out, x, gate)


def _final_norm_kernel(x_ref, g_ref, o_ref):
    x = x_ref[...]
    ms = jnp.mean(x * x, axis=-1, keepdims=True)
    o_ref[...] = x * lax.rsqrt(ms + EPS) * g_ref[...]


def _final_norm(x, g):
    b, s, d = x.shape
    ts = _pick(s, 256)
    return pl.pallas_call(
        _final_norm_kernel,
        grid=(b, s // ts),
        in_specs=[pl.BlockSpec((None, ts, d), lambda i, j: (i, j, 0)),
                  pl.BlockSpec((1, d), lambda i, j: (0, 0))],
        out_specs=pl.BlockSpec((None, ts, d), lambda i, j: (i, j, 0)),
        out_shape=jax.ShapeDtypeStruct((b, s, d), F32),
        compiler_params=_params("arbitrary", "arbitrary"),
        name="final_norm",
    )(x, g)


def kernel(x, c, w_ada, b_ada, g_norm, w_in, g_sb, g_dil, w_out, g_final):
    b, s, d = x.shape
    depth = w_ada.shape[0]
    n_sb = g_sb.shape[1] // HEAD_DIM
    n_dil = g_dil.shape[1] // HEAD_DIM
    sb_w = n_sb * HEAD_DIM
    dil_w = n_dil * HEAD_DIM
    assert s % DIL_PAIRS[-1][0] == 0 and sb_w == dil_w

    qscale = LOG2E / math.sqrt(HEAD_DIM)
    colscale = jnp.concatenate([
        jnp.full((sb_w,), qscale, F32), jnp.ones((3 * sb_w,), F32),
        jnp.full((dil_w,), qscale, F32), jnp.ones((3 * dil_w,), F32)])[None, :]
    slopes = jnp.exp2(-ALIBI_MAX_BIAS * jnp.arange(1, n_dil + 1, dtype=F32) / n_dil) * LOG2E
    slopes = jnp.broadcast_to(slopes[:, None, None], (n_dil, 1, HEAD_DIM))

    pad = 16
    c_pad = jnp.zeros((pad, d), F32).at[:b].set(c.astype(F32))
    for layer in range(depth):
        mod = _ada(c_pad, w_ada[layer], b_ada[layer][None, :])[:b]
        shift, scale, gate = (mod[:, i * d:(i + 1) * d][:, None, :] for i in range(3))
        h = _norm_mod(x, g_norm[layer][None, :], scale, shift)
        proj_sb = _inproj(h, w_in[layer], colscale, 0, 4 * sb_w, 1)
        proj_dl = _inproj(h, w_in[layer], colscale, 4 * sb_w, 4 * dil_w, DIL_PERM)
        y_sb = _sb_attention(proj_sb, g_sb[layer][None, :], n_sb)
        y_dl = _dil_attention(proj_dl, g_dil[layer][None, :], slopes, n_dil)
        x = _outproj(y_sb, y_dl, w_out[layer], x, gate)
    return _final_norm(x, g_final[None, :])
```

```python
import functools
import math

import jax
import jax.numpy as jnp
from jax import lax
from jax.experimental import pallas as pl
from jax.experimental.pallas import tpu as pltpu

HEAD_DIM = 128
EPS = 1e-6
LOG2E = 1.4426950408889634
ALIBI_MAX_BIAS = 8.0
DIL_PAIRS = ((128, 1), (512, 4), (2048, 16))
DIL_BLOCK = 128
DIL_PERM = 4
MASK_BIAS = -1e30
V7X_VMEM_BYTES = 64 * 1024 * 1024
VMEM_LIMIT = V7X_VMEM_BYTES - 8 * 1024 * 1024
IN_PROJ_TM, IN_PROJ_TN = 1024, 512
OUT_PROJ_TM, OUT_PROJ_TN = 1024, 512
NORM_ROWS = 512

F32 = jnp.float32
BF16 = jnp.bfloat16


def _params(*sem):
    return pltpu.CompilerParams(dimension_semantics=sem, vmem_limit_bytes=VMEM_LIMIT)


def _pick(n, pref):
    t = min(pref, n)
    while n % t:
        t //= 2
    return t


def _silu(x):
    return x / (1.0 + jnp.exp(-x))


def _dot_nt(a, b):
    return lax.dot_general(a, b, (((1,), (1,)), ((), ())), preferred_element_type=F32)


def _ada_kernel(c_ref, w_ref, b_ref, o_ref):
    cs = _silu(c_ref[...])
    acc = jnp.dot(cs.astype(BF16), w_ref[...].astype(BF16), preferred_element_type=F32)
    o_ref[...] = acc + b_ref[...]


def _ada(c_pad, w, b):
    m, d = c_pad.shape
    n = w.shape[1]
    tn = _pick(n, 512)
    return pl.pallas_call(
        _ada_kernel,
        grid=(n // tn,),
        in_specs=[
            pl.BlockSpec((m, d), lambda j: (0, 0)),
            pl.BlockSpec((d, tn), lambda j: (0, j)),
            pl.BlockSpec((1, tn), lambda j: (0, j)),
        ],
        out_specs=pl.BlockSpec((m, tn), lambda j: (0, j)),
        out_shape=jax.ShapeDtypeStruct((m, n), F32),
        compiler_params=_params("arbitrary"),
        name="ada_ln",
    )(c_pad, w, b)


def _norm_mod_kernel(x_ref, g_ref, scale_ref, shift_ref, h_ref, gs_ref):
    @pl.when(pl.program_id(1) == 0)
    def _():
        gs_ref[...] = g_ref[...] * (1.0 + scale_ref[...])

    x = x_ref[...]
    ms = jnp.mean(x * x, axis=-1, keepdims=True)
    h_ref[...] = (x * lax.rsqrt(ms + EPS) * gs_ref[...] + shift_ref[...]).astype(BF16)


def _norm_mod(x, g, scale, shift):
    b, s, d = x.shape
    ts = _pick(s, NORM_ROWS)
    return pl.pallas_call(
        _norm_mod_kernel,
        grid=(b, s // ts),
        in_specs=[
            pl.BlockSpec((None, ts, d), lambda i, j: (i, j, 0)),
            pl.BlockSpec((1, d), lambda i, j: (0, 0)),
            pl.BlockSpec((None, 1, d), lambda i, j: (i, 0, 0)),
            pl.BlockSpec((None, 1, d), lambda i, j: (i, 0, 0)),
        ],
        out_specs=pl.BlockSpec((None, ts, d), lambda i, j: (i, j, 0)),
        out_shape=jax.ShapeDtypeStruct((b, s, d), BF16),
        scratch_shapes=[pltpu.VMEM((1, d), F32)],
        compiler_params=_params("arbitrary", "arbitrary"),
        name="norm_mod",
    )(x, g, scale, shift)


def _inproj_kernel(h_ref, w_ref, cs_ref, o_ref, wb_ref, *maybe_slabs, perm):
    @pl.when(pl.program_id(1) == 0)
    def _():
        wb_ref[...] = (w_ref[...] * cs_ref[...]).astype(BF16)

    vals = jnp.dot(h_ref[...], wb_ref[...], preferred_element_type=F32)
    if perm == 1:
        o_ref[...] = vals.astype(BF16)
    else:
        (slab_ref,) = maybe_slabs
        tm, tn = vals.shape
        lanes = slab_ref.shape[2]
        for slab in range(tn // lanes):
            slab_ref[slab] = vals[:, slab * lanes:(slab + 1) * lanes]
        for res in range(perm):
            for slab in range(tn // lanes):
                rows = slab_ref[slab, pl.ds(res, tm // perm, stride=perm), :]
                o_ref[res, :, slab * lanes:(slab + 1) * lanes] = rows.astype(BF16)


def _inproj(h, w, colscale, col0, ncols, perm):
    b, s, d = h.shape
    tm = _pick(s, IN_PROJ_TM)
    tn = _pick(ncols, IN_PROJ_TN)
    nm = s // tm
    joff = col0 // tn
    assert col0 % tn == 0 and tm % (8 * perm) == 0
    in_specs = [
        pl.BlockSpec((None, tm, d), lambda j, i: (i // nm, i % nm, 0)),
        pl.BlockSpec((d, tn), lambda j, i: (0, joff + j)),
        pl.BlockSpec((1, tn), lambda j, i: (0, joff + j)),
    ]
    scratch = [pltpu.VMEM((d, tn), BF16)]
    if perm == 1:
        out_spec = pl.BlockSpec((None, tm, tn), lambda j, i: (i // nm, i % nm, j))
        out_shape = (b, s, ncols)
    else:
        out_spec = pl.BlockSpec((None, perm, tm // perm, tn), lambda j, i: (i // nm, 0, i % nm, j))
        out_shape = (b, perm, s // perm, ncols)
        scratch.append(pltpu.VMEM((tn // HEAD_DIM, tm, HEAD_DIM), F32))
    out = pl.pallas_call(
        functools.partial(_inproj_kernel, perm=perm),
        grid=(ncols // tn, b * nm),
        in_specs=in_specs,
        out_specs=out_spec,
        out_shape=jax.ShapeDtypeStruct(out_shape, BF16),
        scratch_shapes=scratch,
        compiler_params=_params("arbitrary", "arbitrary"),
        name="in_proj_p%d" % perm,
    )(h, w, colscale)
    return out.reshape(b, s, ncols)


def _head_epilogue(o, g, zg):
    ms = jnp.mean(o * o, axis=-1, keepdims=True)
    return o * lax.rsqrt(ms + EPS) * g * _silu(zg.astype(F32))


def _sb_kernel(q_ref, k_ref, v_ref, z_ref, g_ref, o_ref, acc_ref, carry_ref, *, seq, tile):
    row = lax.broadcasted_iota(jnp.int32, (tile, tile), 0)
    col = lax.broadcasted_iota(jnp.int32, (tile, tile), 1)
    causal = col < row
    upper = jnp.where(row > col, 1.0, 0.0).astype(BF16)

    def key_tile(q, k, v, carry, mask):
        s = _dot_nt(q, k)
        e = jnp.exp2(-jnp.abs(s))
        lb = jnp.minimum(s, 0.0) - jnp.log(1.0 + e) * LOG2E
        lom = lb - s
        if mask is not None:
            lom = jnp.where(mask, lom, 0.0)
        arg = lb + jnp.dot(lom.astype(BF16), upper, preferred_element_type=F32)
        if carry is not None:
            arg = arg + carry
        a = jnp.exp2(arg)
        if mask is not None:
            a = jnp.where(mask, a, 0.0)
        av = jnp.dot(a.astype(BF16), v, preferred_element_type=F32)
        return av, jnp.sum(lom, axis=1, keepdims=True)

    for kj in reversed(range(seq // tile)):
        k0 = kj * tile
        k = k_ref[pl.ds(k0, tile), :]
        v = v_ref[pl.ds(k0, tile), :]
        av, rsum = key_tile(q_ref[pl.ds(k0, tile), :], k, v, None, causal)
        acc_ref[pl.ds(k0, tile), :] = av
        carry_ref[pl.ds(k0, tile), :] = rsum
        if k0 + tile < seq:
            later = pl.ds(k0 + tile, seq - k0 - tile)
            av, rsum = key_tile(q_ref[later, :], k, v, carry_ref[later, :], None)
            acc_ref[later, :] += av
            carry_ref[later, :] += rsum

    for q0 in range(0, seq, tile):
        rows = pl.ds(q0, tile)
        o_ref[rows, :] = _head_epilogue(acc_ref[rows, :], g_ref[...], z_ref[rows, :]).astype(BF16)


def _head_spec(s, n_heads, off):
    return pl.BlockSpec((None, s, HEAD_DIM), lambda i, h: (i, 0, off * n_heads + h))


def _sb_attention(proj, g_sb, n_heads):
    b, s, _ = proj.shape
    tile = _pick(s, 256)
    return pl.pallas_call(
        functools.partial(_sb_kernel, seq=s, tile=tile),
        grid=(b, n_heads),
        in_specs=[_head_spec(s, n_heads, 0), _head_spec(s, n_heads, 1), _head_spec(s, n_heads, 2),
                  _head_spec(s, n_heads, 3), pl.BlockSpec((1, HEAD_DIM), lambda i, h: (0, h))],
        out_specs=pl.BlockSpec((None, s, HEAD_DIM), lambda i, h: (i, 0, h)),
        out_shape=jax.ShapeDtypeStruct((b, s, n_heads * HEAD_DIM), BF16),
        scratch_shapes=[pltpu.VMEM((s, HEAD_DIM), F32), pltpu.VMEM((s, 1), F32)],
        compiler_params=_params("arbitrary", "arbitrary"),
        name="sb_attn",
    )(proj, proj, proj, proj, g_sb)


def _dil_kernel(q_ref, k_ref, v_ref, z_ref, g_ref, slope_ref, o_ref,
                qf, kf, vf, q16, k16, vo, vo16, num_s, den_s, max_s, onat, *, seq):
    n = DIL_BLOCK
    seg = seq // DIL_PERM
    sub = n // DIL_PERM
    slope = slope_ref[0:1, 0:1]

    def bias_of(dist, window):
        return jnp.where((dist >= 0) & (dist <= window), dist.astype(F32) * (-slope), MASK_BIAS)

    def iota2(nk):
        return (lax.broadcasted_iota(jnp.int32, (n, nk), 0), lax.broadcasted_iota(jnp.int32, (n, nk), 1))

    def attend(q, k, vv, bias):
        s = _dot_nt(q, k) + bias
        m = jnp.max(s, axis=1, keepdims=True)
        p = jnp.exp2(s - m)
        pv = jnp.dot(p.astype(BF16), vv, preferred_element_type=F32)
        return pv[:, :HEAD_DIM], pv[:, HEAD_DIM:], jnp.broadcast_to(m, (n, HEAD_DIM))

    def put(br, idx, res):
        num_s[br, idx, :], den_s[br, idx, :], max_s[br, idx, :] = res

    ones = jnp.ones((seq, HEAD_DIM), BF16)
    vo[:, :HEAD_DIM] = v_ref[...]
    vo[:, HEAD_DIM:] = ones
    vo16[:, HEAD_DIM:] = ones

    a, c = iota2(2 * n)
    tq = (a % sub) * DIL_PERM + a // sub
    bias_prev = bias_of(tq - ((c % (2 * sub)) * DIL_PERM + c // (2 * sub) - n), DIL_PAIRS[0][0])
    a, c = iota2(n)
    bias_first = bias_of(tq[:, :n] - ((c % sub) * DIL_PERM + c // sub), DIL_PAIRS[0][0])

    def gather(ref, start, size):
        return jnp.concatenate([ref[pl.ds(res * seg + start, size), :] for res in range(DIL_PERM)], axis=0)

    for i in range(seq // n):
        q = gather(q_ref, sub * i, sub)
        if i == 0:
            res3 = attend(q, gather(k_ref, 0, sub), gather(vo, 0, sub), bias_first)
        else:
            res3 = attend(q, gather(k_ref, sub * (i - 1), 2 * sub), gather(vo, sub * (i - 1), 2 * sub), bias_prev)
        for res in range(DIL_PERM):
            put(0, pl.ds(res * seg + sub * i, sub), tuple(t[res * sub:(res + 1) * sub] for t in res3))

    r = DIL_PAIRS[1][1]
    assert r == DIL_PERM
    a, c = iota2(2 * n)
    bias_prev = bias_of(r * (a - c + n), DIL_PAIRS[1][0])
    bias_first = bias_prev[:, n:]
    r16 = DIL_PAIRS[2][1]
    step = r16 // DIL_PERM
    assert seq // r16 == n
    a, c = iota2(n)
    bias_16 = bias_of(r16 * (a - c), DIL_PAIRS[2][0])
    chunk = _pick(seg, 256)

    for res in range(DIL_PERM):
        grp = pl.ds(res * seg, seg)
        for i in range(seg // n):
            row0 = res * seg + i * n
            q = q_ref[pl.ds(row0, n), :]
            if i == 0:
                res3 = attend(q, k_ref[pl.ds(row0, n), :], vo[pl.ds(row0, n), :], bias_first)
            else:
                res3 = attend(q, k_ref[pl.ds(row0 - n, 2 * n), :], vo[pl.ds(row0 - n, 2 * n), :], bias_prev)
            put(1, pl.ds(row0, n), res3)

        qf[grp, :] = q_ref[grp, :].astype(F32)
        kf[grp, :] = k_ref[grp, :].astype(F32)
        vf[grp, :] = v_ref[grp, :].astype(F32)
        for sres in range(step):
            src = pl.ds(res * seg + sres, n, stride=step)
            blk = pl.ds((res * step + sres) * n, n)
            q16[blk, :] = qf[src, :].astype(BF16)
            k16[blk, :] = kf[src, :].astype(BF16)
            vo16[blk, :HEAD_DIM] = vf[src, :].astype(BF16)
            put(2, src, attend(q16[blk, :], k16[blk, :], vo16[blk, :], bias_16))

        for l0 in range(0, seg, chunk):
            rows = pl.ds(res * seg + l0, chunk)
            m_all = jnp.maximum(jnp.maximum(max_s[0, rows, :], max_s[1, rows, :]), max_s[2, rows, :])
            num = jnp.zeros((chunk, HEAD_DIM), F32)
            den = jnp.zeros((chunk, HEAD_DIM), F32)
            for br in range(len(DIL_PAIRS)):
                w = jnp.exp2(max_s[br, rows, :] - m_all)
                num = num + num_s[br, rows, :] * w
                den = den + den_s[br, rows, :] * w
            onat[pl.ds(DIL_PERM * l0 + res, chunk, stride=DIL_PERM), :] = _head_epilogue(
                num / den, g_ref[...], z_ref[rows, :])
    for c0 in range(0, seq, chunk):
        o_ref[pl.ds(c0, chunk), :] = onat[pl.ds(c0, chunk), :].astype(BF16)


def _dil_attention(proj, g_dil, slopes, n_heads):
    b, s, _ = proj.shape
    nbr = len(DIL_PAIRS)
    rows_f32 = pltpu.VMEM((s, HEAD_DIM), F32)
    rows_bf16 = pltpu.VMEM((s, HEAD_DIM), BF16)
    rows2_bf16 = pltpu.VMEM((s, 2 * HEAD_DIM), BF16)
    per_branch = pltpu.VMEM((nbr, s, HEAD_DIM), F32)
    return pl.pallas_call(
        functools.partial(_dil_kernel, seq=s),
        grid=(b, n_heads),
        in_specs=[_head_spec(s, n_heads, 0), _head_spec(s, n_heads, 1), _head_spec(s, n_heads, 2),
                  _head_spec(s, n_heads, 3), pl.BlockSpec((1, HEAD_DIM), lambda i, h: (0, h)),
                  pl.BlockSpec((None, 1, HEAD_DIM), lambda i, h: (h, 0, 0))],
        out_specs=pl.BlockSpec((None, s, HEAD_DIM), lambda i, h: (i, 0, h)),
        out_shape=jax.ShapeDtypeStruct((b, s, n_heads * HEAD_DIM), BF16),
        scratch_shapes=[rows_f32, rows_f32, rows_f32, rows_bf16, rows_bf16, rows2_bf16, rows2_bf16,
                        per_branch, per_branch, per_branch, rows_f32],
        compiler_params=_params("arbitrary", "arbitrary"),
        name="dil_attn",
    )(proj, proj, proj, proj, g_dil, slopes)


def _outproj_kernel(ya_ref, yb_ref, wa_ref, wb_ref, x_ref, gate_ref, o_ref, wab_ref, wbb_ref):
    @pl.when(pl.program_id(1) == 0)
    def _():
        wab_ref[...] = wa_ref[...].astype(BF16)
        wbb_ref[...] = wb_ref[...].astype(BF16)

    acc = jnp.dot(ya_ref[...], wab_ref[...], preferred_element_type=F32)
    acc = acc + jnp.dot(yb_ref[...], wbb_ref[...], preferred_element_type=F32)
    o_ref[...] = x_ref[...] + gate_ref[...] * acc


def _outproj(y_sb, y_dl, w_out, x, gate):
    b, s, d = x.shape
    ka = y_sb.shape[2]
    kb = y_dl.shape[2]
    tm = _pick(s, OUT_PROJ_TM)
    tn = _pick(d, OUT_PROJ_TN)
    nm = s // tm
    return pl.pallas_call(
        _outproj_kernel,
        grid=(d // tn, b * nm),
        in_specs=[
            pl.BlockSpec((None, tm, ka), lambda j, i: (i // nm, i % nm, 0)),
            pl.BlockSpec((None, tm, kb), lambda j, i: (i // nm, i % nm, 0)),
            pl.BlockSpec((ka, tn), lambda j, i: (0, j)),
            pl.BlockSpec((kb, tn), lambda j, i: (ka // kb, j)),
            pl.BlockSpec((None, tm, tn), lambda j, i: (i // nm, i % nm, j)),
            pl.BlockSpec((None, 1, tn), lambda j, i: (i // nm, 0, j)),
        ],
        out_specs=pl.BlockSpec((None, tm, tn), lambda j, i: (i // nm, i % nm, j)),
        out_shape=jax.ShapeDtypeStruct((b, s, d), F32),
        scratch_shapes=[pltpu.VMEM((ka, tn), BF16), pltpu.VMEM((kb, tn), BF16)],
        compiler_params=_params("arbitrary", "arbitrary"),
        name="out_proj",
    )(y_sb, y_dl, w_out, w_out, x, gate)


def _final_norm_kernel(x_ref, g_ref, o_ref):
    x = x_ref[...]
    ms = jnp.mean(x * x, axis=-1, keepdims=True)
    o_ref[...] = x * lax.rsqrt(ms + EPS) * g_ref[...]


def _final_norm(x, g):
    b, s, d = x.shape
    ts = _pick(s, NORM_ROWS)
    return pl.pallas_call(
        _final_norm_kernel,
        grid=(b, s // ts),
        in_specs=[pl.BlockSpec((None, ts, d), lambda i, j: (i, j, 0)),
                  pl.BlockSpec((1, d), lambda i, j: (0, 0))],
        out_specs=pl.BlockSpec((None, ts, d), lambda i, j: (i, j, 0)),
        out_shape=jax.ShapeDtypeStruct((b, s, d), F32),
        compiler_params=_params("arbitrary", "arbitrary"),
        name="final_norm",
    )(x, g)


def kernel(x, c, w_ada, b_ada, g_norm, w_in, g_sb, g_dil, w_out, g_final):
    b, s, d = x.shape
    depth = w_ada.shape[0]
    n_sb = g_sb.shape[1] // HEAD_DIM
    n_dil = g_dil.shape[1] // HEAD_DIM
    sb_w = n_sb * HEAD_DIM
    dil_w = n_dil * HEAD_DIM
    assert s % DIL_PAIRS[-1][0] == 0 and sb_w == dil_w

    qscale = LOG2E / math.sqrt(HEAD_DIM)
    colscale = jnp.concatenate([
        jnp.full((sb_w,), qscale, F32), jnp.ones((3 * sb_w,), F32),
        jnp.full((dil_w,), qscale, F32), jnp.ones((3 * dil_w,), F32)])[None, :]
    slopes = jnp.exp2(-ALIBI_MAX_BIAS * jnp.arange(1, n_dil + 1, dtype=F32) / n_dil) * LOG2E
    slopes = jnp.broadcast_to(slopes[:, None, None], (n_dil, 1, HEAD_DIM))

    pad = 16
    c_pad = jnp.zeros((pad, d), F32).at[:b].set(c.astype(F32))
    for layer in range(depth):
        mod = _ada(c_pad, w_ada[layer], b_ada[layer][None, :])[:b]
        shift, scale, gate = (mod[:, i * d:(i + 1) * d][:, None, :] for i in range(3))
        h = _norm_mod(x, g_norm[layer][None, :], scale, shift)
        proj_sb = _inproj(h, w_in[layer], colscale, 0, 4 * sb_w, 1)
        proj_dl = _inproj(h, w_in[layer], colscale, 4 * sb_w, 4 * dil_w, DIL_PERM)
        y_sb = _sb_attention(proj_sb, g_sb[layer][None, :], n_sb)
        y_dl = _dil_attention(proj_dl, g_dil[layer][None, :], slopes, n_dil)
        x = _outproj(y_sb, y_dl, w_out[layer], x, gate)
    return _final_norm(x, g_final[None, :])
```

```python
import functools
import math

import jax
import jax.numpy as jnp
from jax import lax
from jax.experimental import pallas as pl
from jax.experimental.pallas import tpu as pltpu

HEAD_DIM = 128
EPS = 1e-6
LOG2E = 1.4426950408889634
ALIBI_MAX_BIAS = 8.0
DIL_PAIRS = ((128, 1), (512, 4), (2048, 16))
DIL_BLOCK = 128
DIL_PERM = 4
MASK_BIAS = -1e30
V7X_VMEM_BYTES = 64 * 1024 * 1024
VMEM_LIMIT = V7X_VMEM_BYTES - 8 * 1024 * 1024
IN_PROJ_TM, IN_PROJ_TN = 1024, 1024
OUT_PROJ_TM, OUT_PROJ_TN = 1024, 512
NORM_ROWS = 512

F32 = jnp.float32
BF16 = jnp.bfloat16


def _params(*sem):
    return pltpu.CompilerParams(dimension_semantics=sem, vmem_limit_bytes=VMEM_LIMIT)


def _pick(n, pref):
    t = min(pref, n)
    while n % t:
        t //= 2
    return t


def _silu(x):
    return x / (1.0 + jnp.exp(-x))


def _dot_nt(a, b):
    return lax.dot_general(a, b, (((1,), (1,)), ((), ())), preferred_element_type=F32)


def _ada_kernel(c_ref, w_ref, b_ref, o_ref):
    cs = _silu(c_ref[...])
    acc = jnp.dot(cs.astype(BF16), w_ref[...].astype(BF16), preferred_element_type=F32)
    o_ref[...] = acc + b_ref[...]


def _ada(c_pad, w, b):
    m, d = c_pad.shape
    n = w.shape[1]
    tn = _pick(n, 512)
    return pl.pallas_call(
        _ada_kernel,
        grid=(n // tn,),
        in_specs=[
            pl.BlockSpec((m, d), lambda j: (0, 0)),
            pl.BlockSpec((d, tn), lambda j: (0, j)),
            pl.BlockSpec((1, tn), lambda j: (0, j)),
        ],
        out_specs=pl.BlockSpec((m, tn), lambda j: (0, j)),
        out_shape=jax.ShapeDtypeStruct((m, n), F32),
        compiler_params=_params("arbitrary"),
        name="ada_ln",
    )(c_pad, w, b)


def _norm_mod_kernel(x_ref, g_ref, scale_ref, shift_ref, h_ref, gs_ref):
    @pl.when(pl.program_id(1) == 0)
    def _():
        gs_ref[...] = g_ref[...] * (1.0 + scale_ref[...])

    x = x_ref[...]
    ms = jnp.mean(x * x, axis=-1, keepdims=True)
    h_ref[...] = (x * lax.rsqrt(ms + EPS) * gs_ref[...] + shift_ref[...]).astype(BF16)


def _norm_mod(x, g, scale, shift):
    b, s, d = x.shape
    ts = _pick(s, NORM_ROWS)
    return pl.pallas_call(
        _norm_mod_kernel,
        grid=(b, s // ts),
        in_specs=[
            pl.BlockSpec((None, ts, d), lambda i, j: (i, j, 0)),
            pl.BlockSpec((1, d), lambda i, j: (0, 0)),
            pl.BlockSpec((None, 1, d), lambda i, j: (i, 0, 0)),
            pl.BlockSpec((None, 1, d), lambda i, j: (i, 0, 0)),
        ],
        out_specs=pl.BlockSpec((None, ts, d), lambda i, j: (i, j, 0)),
        out_shape=jax.ShapeDtypeStruct((b, s, d), BF16),
        scratch_shapes=[pltpu.VMEM((1, d), F32)],
        compiler_params=_params("arbitrary", "arbitrary"),
        name="norm_mod",
    )(x, g, scale, shift)


def _inproj_kernel(h_ref, wk_ref, cs_ref, o_ref, wb_ref, *maybe_slabs, perm):
    j = pl.program_id(0)
    i = pl.program_id(1)
    kc = wk_ref.shape[0]

    def cast_chunk():
        rows = pl.ds(pl.multiple_of(i * kc, kc), kc)
        wb_ref[j % 2, rows, :] = (wk_ref[...] * cs_ref[...]).astype(BF16)

    @pl.when(j == 0)
    def _():
        cast_chunk()

    @pl.when(j > 0)
    def _():
        cast_chunk()
        vals = jnp.dot(h_ref[...], wb_ref[(j - 1) % 2], preferred_element_type=F32)
        if perm == 1:
            o_ref[...] = vals.astype(BF16)
        else:
            (slab_ref,) = maybe_slabs
            tm, tn = vals.shape
            lanes = slab_ref.shape[2]
            for slab in range(tn // lanes):
                slab_ref[slab] = vals[:, slab * lanes:(slab + 1) * lanes]
            for res in range(perm):
                for slab in range(tn // lanes):
                    rows = slab_ref[slab, pl.ds(res, tm // perm, stride=perm), :]
                    o_ref[res, :, slab * lanes:(slab + 1) * lanes] = rows.astype(BF16)


def _inproj(h, w, colscale, col0, ncols, perm):
    b, s, d = h.shape
    tm = _pick(s, IN_PROJ_TM)
    tn = _pick(ncols, IN_PROJ_TN)
    nm = s // tm
    steps = b * nm
    ncb = ncols // tn
    joff = col0 // tn
    kc = d // steps
    assert col0 % tn == 0 and tm % (8 * perm) == 0 and d % steps == 0 and kc % 16 == 0

    def row(j, i):
        i = jnp.where(j == 0, 0, i)
        return i // nm, i % nm

    def next_col(j):
        return joff + jnp.minimum(j, ncb - 1)

    in_specs = [
        pl.BlockSpec((None, tm, d), lambda j, i: (*row(j, i), 0)),
        pl.BlockSpec((kc, tn), lambda j, i: (i, next_col(j))),
        pl.BlockSpec((1, tn), lambda j, i: (0, next_col(j))),
    ]
    scratch = [pltpu.VMEM((2, d, tn), BF16)]
    if perm == 1:
        out_spec = pl.BlockSpec((None, tm, tn), lambda j, i: (*row(j, i), jnp.maximum(j - 1, 0)))
        out_shape = (b, s, ncols)
    else:
        out_spec = pl.BlockSpec((None, perm, tm // perm, tn),
                                lambda j, i: (row(j, i)[0], 0, row(j, i)[1], jnp.maximum(j - 1, 0)))
        out_shape = (b, perm, s // perm, ncols)
        scratch.append(pltpu.VMEM((tn // HEAD_DIM, tm, HEAD_DIM), F32))
    out = pl.pallas_call(
        functools.partial(_inproj_kernel, perm=perm),
        grid=(ncb + 1, steps),
        in_specs=in_specs,
        out_specs=out_spec,
        out_shape=jax.ShapeDtypeStruct(out_shape, BF16),
        scratch_shapes=scratch,
        compiler_params=_params("arbitrary", "arbitrary"),
        name="in_proj_p%d" % perm,
    )(h, w, colscale)
    return out.reshape(b, s, ncols)


def _head_epilogue(o, g, zg):
    ms = jnp.mean(o * o, axis=-1, keepdims=True)
    return o * lax.rsqrt(ms + EPS) * g * _silu(zg.astype(F32))


def _sb_kernel(q_ref, k_ref, v_ref, z_ref, g_ref, o_ref, acc_ref, carry_ref, *, seq, tile):
    row = lax.broadcasted_iota(jnp.int32, (tile, tile), 0)
    col = lax.broadcasted_iota(jnp.int32, (tile, tile), 1)
    causal = col < row
    upper = jnp.where(row > col, 1.0, 0.0).astype(BF16)

    def key_tile(q, k, v, carry, mask):
        s = _dot_nt(q, k)
        e = jnp.exp2(-jnp.abs(s))
        lb = jnp.minimum(s, 0.0) - jnp.log(1.0 + e) * LOG2E
        lom = lb - s
        if mask is not None:
            lom = jnp.where(mask, lom, 0.0)
        arg = lb + jnp.dot(lom.astype(BF16), upper, preferred_element_type=F32)
        if carry is not None:
            arg = arg + carry
        a = jnp.exp2(arg)
        if mask is not None:
            a = jnp.where(mask, a, 0.0)
        av = jnp.dot(a.astype(BF16), v, preferred_element_type=F32)
        return av, jnp.sum(lom, axis=1, keepdims=True)

    for kj in reversed(range(seq // tile)):
        k0 = kj * tile
        k = k_ref[pl.ds(k0, tile), :]
        v = v_ref[pl.ds(k0, tile), :]
        av, rsum = key_tile(q_ref[pl.ds(k0, tile), :], k, v, None, causal)
        acc_ref[pl.ds(k0, tile), :] = av
        carry_ref[pl.ds(k0, tile), :] = rsum
        if k0 + tile < seq:
            later = pl.ds(k0 + tile, seq - k0 - tile)
            av, rsum = key_tile(q_ref[later, :], k, v, carry_ref[later, :], None)
            acc_ref[later, :] += av
            carry_ref[later, :] += rsum

    for q0 in range(0, seq, tile):
        rows = pl.ds(q0, tile)
        o_ref[rows, :] = _head_epilogue(acc_ref[rows, :], g_ref[...], z_ref[rows, :]).astype(BF16)


def _head_spec(s, n_heads, off):
    return pl.BlockSpec((None, s, HEAD_DIM), lambda i, h: (i, 0, off * n_heads + h))


def _sb_attention(proj, g_sb, n_heads):
    b, s, _ = proj.shape
    tile = _pick(s, 256)
    return pl.pallas_call(
        functools.partial(_sb_kernel, seq=s, tile=tile),
        grid=(b, n_heads),
        in_specs=[_head_spec(s, n_heads, 0), _head_spec(s, n_heads, 1), _head_spec(s, n_heads, 2),
                  _head_spec(s, n_heads, 3), pl.BlockSpec((1, HEAD_DIM), lambda i, h: (0, h))],
        out_specs=pl.BlockSpec((None, s, HEAD_DIM), lambda i, h: (i, 0, h)),
        out_shape=jax.ShapeDtypeStruct((b, s, n_heads * HEAD_DIM), BF16),
        scratch_shapes=[pltpu.VMEM((s, HEAD_DIM), F32), pltpu.VMEM((s, 1), F32)],
        compiler_params=_params("arbitrary", "arbitrary"),
        name="sb_attn",
    )(proj, proj, proj, proj, g_sb)


def _dil_kernel(q_ref, k_ref, v_ref, z_ref, g_ref, slope_ref, o_ref,
                qf, kf, vf, q16, k16, vo, vo16, num_s, den_s, max_s, onat, *, seq):
    n = DIL_BLOCK
    seg = seq // DIL_PERM
    sub = n // DIL_PERM
    slope = slope_ref[0:1, 0:1]

    def bias_of(dist, window):
        return jnp.where((dist >= 0) & (dist <= window), dist.astype(F32) * (-slope), MASK_BIAS)

    def iota2(nk):
        return (lax.broadcasted_iota(jnp.int32, (n, nk), 0), lax.broadcasted_iota(jnp.int32, (n, nk), 1))

    def attend(q, k, vv, bias):
        s = _dot_nt(q, k) + bias
        m = jnp.max(s, axis=1, keepdims=True)
        p = jnp.exp2(s - m)
        pv = jnp.dot(p.astype(BF16), vv, preferred_element_type=F32)
        return pv[:, :HEAD_DIM], pv[:, HEAD_DIM:], jnp.broadcast_to(m, (n, HEAD_DIM))

    def put(br, idx, res):
        num_s[br, idx, :], den_s[br, idx, :], max_s[br, idx, :] = res

    ones = jnp.ones((seq, HEAD_DIM), BF16)
    vo[:, :HEAD_DIM] = v_ref[...]
    vo[:, HEAD_DIM:] = ones
    vo16[:, HEAD_DIM:] = ones

    a, c = iota2(2 * n)
    tq = (a % sub) * DIL_PERM + a // sub
    bias_prev = bias_of(tq - ((c % (2 * sub)) * DIL_PERM + c // (2 * sub) - n), DIL_PAIRS[0][0])
    a, c = iota2(n)
    bias_first = bias_of(tq[:, :n] - ((c % sub) * DIL_PERM + c // sub), DIL_PAIRS[0][0])

    def gather(ref, start, size):
        return jnp.concatenate([ref[pl.ds(res * seg + start, size), :] for res in range(DIL_PERM)], axis=0)

    for i in range(seq // n):
        q = gather(q_ref, sub * i, sub)
        if i == 0:
            res3 = attend(q, gather(k_ref, 0, sub), gather(vo, 0, sub), bias_first)
        else:
            res3 = attend(q, gather(k_ref, sub * (i - 1), 2 * sub), gather(vo, sub * (i - 1), 2 * sub), bias_prev)
        for res in range(DIL_PERM):
            put(0, pl.ds(res * seg + sub * i, sub), tuple(t[res * sub:(res + 1) * sub] for t in res3))

    r = DIL_PAIRS[1][1]
    assert r == DIL_PERM
    a, c = iota2(2 * n)
    bias_prev = bias_of(r * (a - c + n), DIL_PAIRS[1][0])
    bias_first = bias_prev[:, n:]
    r16 = DIL_PAIRS[2][1]
    step = r16 // DIL_PERM
    assert seq // r16 == n
    a, c = iota2(n)
    bias_16 = bias_of(r16 * (a - c), DIL_PAIRS[2][0])
    chunk = _pick(seg, 256)

    for res in range(DIL_PERM):
        grp = pl.ds(res * seg, seg)
        for i in range(seg // n):
            row0 = res * seg + i * n
            q = q_ref[pl.ds(row0, n), :]
            if i == 0:
                res3 = attend(q, k_ref[pl.ds(row0, n), :], vo[pl.ds(row0, n), :], bias_first)
            else:
                res3 = attend(q, k_ref[pl.ds(row0 - n, 2 * n), :], vo[pl.ds(row0 - n, 2 * n), :], bias_prev)
            put(1, pl.ds(row0, n), res3)

        qf[grp, :] = q_ref[grp, :].astype(F32)
        kf[grp, :] = k_ref[grp, :].astype(F32)
        vf[grp, :] = v_ref[grp, :].astype(F32)
        for sres in range(step):
            src = pl.ds(res * seg + sres, n, stride=step)
            blk = pl.ds((res * step + sres) * n, n)
            q16[blk, :] = qf[src, :].astype(BF16)
            k16[blk, :] = kf[src, :].astype(BF16)
            vo16[blk, :HEAD_DIM] = vf[src, :].astype(BF16)
            put(2, src, attend(q16[blk, :], k16[blk, :], vo16[blk, :], bias_16))

        for l0 in range(0, seg, chunk):
            rows = pl.ds(res * seg + l0, chunk)
            m_all = jnp.maximum(jnp.maximum(max_s[0, rows, :], max_s[1, rows, :]), max_s[2, rows, :])
            num = jnp.zeros((chunk, HEAD_DIM), F32)
            den = jnp.zeros((chunk, HEAD_DIM), F32)
            for br in range(len(DIL_PAIRS)):
                w = jnp.exp2(max_s[br, rows, :] - m_all)
                num = num + num_s[br, rows, :] * w
                den = den + den_s[br, rows, :] * w
            onat[pl.ds(DIL_PERM * l0 + res, chunk, stride=DIL_PERM), :] = _head_epilogue(
                num / den, g_ref[...], z_ref[rows, :])
    for c0 in range(0, seq, chunk):
        o_ref[pl.ds(c0, chunk), :] = onat[pl.ds(c0, chunk), :].astype(BF16)


def _dil_attention(proj, g_dil, slopes, n_heads):
    b, s, _ = proj.shape
    nbr = len(DIL_PAIRS)
    rows_f32 = pltpu.VMEM((s, HEAD_DIM), F32)
    rows_bf16 = pltpu.VMEM((s, HEAD_DIM), BF16)
    rows2_bf16 = pltpu.VMEM((s, 2 * HEAD_DIM), BF16)
    per_branch = pltpu.VMEM((nbr, s, HEAD_DIM), F32)
    return pl.pallas_call(
        functools.partial(_dil_kernel, seq=s),
        grid=(b, n_heads),
        in_specs=[_head_spec(s, n_heads, 0), _head_spec(s, n_heads, 1), _head_spec(s, n_heads, 2),
                  _head_spec(s, n_heads, 3), pl.BlockSpec((1, HEAD_DIM), lambda i, h: (0, h)),
                  pl.BlockSpec((None, 1, HEAD_DIM), lambda i, h: (h, 0, 0))],
        out_specs=pl.BlockSpec((None, s, HEAD_DIM), lambda i, h: (i, 0, h)),
        out_shape=jax.ShapeDtypeStruct((b, s, n_heads * HEAD_DIM), BF16),
        scratch_shapes=[rows_f32, rows_f32, rows_f32, rows_bf16, rows_bf16, rows2_bf16, rows2_bf16,
                        per_branch, per_branch, per_branch, rows_f32],
        compiler_params=_params("arbitrary", "arbitrary"),
        name="dil_attn",
    )(proj, proj, proj, proj, g_dil, slopes)


def _outproj_kernel(ya_ref, yb_ref, wa_ref, wb_ref, x_ref, gate_ref, o_ref, wab_ref, wbb_ref):
    @pl.when(pl.program_id(1) == 0)
    def _():
        wab_ref[...] = wa_ref[...].astype(BF16)
        wbb_ref[...] = wb_ref[...].astype(BF16)

    acc = jnp.dot(ya_ref[...], wab_ref[...], preferred_element_type=F32)
    acc = acc + jnp.dot(yb_ref[...], wbb_ref[...], preferred_element_type=F32)
    o_ref[...] = x_ref[...] + gate_ref[...] * acc


def _outproj(y_sb, y_dl, w_out, x, gate):
    b, s, d = x.shape
    ka = y_sb.shape[2]
    kb = y_dl.shape[2]
    tm = _pick(s, OUT_PROJ_TM)
    tn = _pick(d, OUT_PROJ_TN)
    nm = s // tm
    return pl.pallas_call(
        _outproj_kernel,
        grid=(d // tn, b * nm),
        in_specs=[
            pl.BlockSpec((None, tm, ka), lambda j, i: (i // nm, i % nm, 0)),
            pl.BlockSpec((None, tm, kb), lambda j, i: (i // nm, i % nm, 0)),
            pl.BlockSpec((ka, tn), lambda j, i: (0, j)),
            pl.BlockSpec((kb, tn), lambda j, i: (ka // kb, j)),
            pl.BlockSpec((None, tm, tn), lambda j, i: (i // nm, i % nm, j)),
            pl.BlockSpec((None, 1, tn), lambda j, i: (i // nm, 0, j)),
        ],
        out_specs=pl.BlockSpec((None, tm, tn), lambda j, i: (i // nm, i % nm, j)),
        out_shape=jax.ShapeDtypeStruct((b, s, d), F32),
        scratch_shapes=[pltpu.VMEM((ka, tn), BF16), pltpu.VMEM((kb, tn), BF16)],
        compiler_params=_params("arbitrary", "arbitrary"),
        name="out_proj",
    )(y_sb, y_dl, w_out, w_out, x, gate)


def _final_norm_kernel(x_ref, g_ref, o_ref):
    x = x_ref[...]
    ms = jnp.mean(x * x, axis=-1, keepdims=True)
    o_ref[...] = x * lax.rsqrt(ms + EPS) * g_ref[...]


def _final_norm(x, g):
    b, s, d = x.shape
    ts = _pick(s, NORM_ROWS)
    return pl.pallas_call(
        _final_norm_kernel,
        grid=(b, s // ts),
        in_specs=[pl.BlockSpec((None, ts, d), lambda i, j: (i, j, 0)),
                  pl.BlockSpec((1, d), lambda i, j: (0, 0))],
        out_specs=pl.BlockSpec((None, ts, d), lambda i, j: (i, j, 0)),
        out_shape=jax.ShapeDtypeStruct((b, s, d), F32),
        compiler_params=_params("arbitrary", "arbitrary"),
        name="final_norm",
    )(x, g)


def kernel(x, c, w_ada, b_ada, g_norm, w_in, g_sb, g_dil, w_out, g_final):
    b, s, d = x.shape
    depth = w_ada.shape[0]
    n_sb = g_sb.shape[1] // HEAD_DIM
    n_dil = g_dil.shape[1] // HEAD_DIM
    sb_w = n_sb * HEAD_DIM
    dil_w = n_dil * HEAD_DIM
    assert s % DIL_PAIRS[-1][0] == 0 and sb_w == dil_w

    qscale = LOG2E / math.sqrt(HEAD_DIM)
    colscale = jnp.concatenate([
        jnp.full((sb_w,), qscale, F32), jnp.ones((3 * sb_w,), F32),
        jnp.full((dil_w,), qscale, F32), jnp.ones((3 * dil_w,), F32)])[None, :]
    slopes = jnp.exp2(-ALIBI_MAX_BIAS * jnp.arange(1, n_dil + 1, dtype=F32) / n_dil) * LOG2E
    slopes = jnp.broadcast_to(slopes[:, None, None], (n_dil, 1, HEAD_DIM))

    pad = 16
    c_pad = jnp.zeros((pad, d), F32).at[:b].set(c.astype(F32))
    for layer in range(depth):
        mod = _ada(c_pad, w_ada[layer], b_ada[layer][None, :])[:b]
        shift, scale, gate = (mod[:, i * d:(i + 1) * d][:, None, :] for i in range(3))
        h = _norm_mod(x, g_norm[layer][None, :], scale, shift)
        proj_sb = _inproj(h, w_in[layer], colscale, 0, 4 * sb_w, 1)
        proj_dl = _inproj(h, w_in[layer], colscale, 4 * sb_w, 4 * dil_w, DIL_PERM)
        y_sb = _sb_attention(proj_sb, g_sb[layer][None, :], n_sb)
        y_dl = _dil_attention(proj_dl, g_dil[layer][None, :], slopes, n_dil)
        x = _outproj(y_sb, y_dl, w_out[layer], x, gate)
    return _final_norm(x, g_final[None, :])
```

```python
import functools
import math

import jax
import jax.numpy as jnp
from jax import lax
from jax.experimental import pallas as pl
from jax.experimental.pallas import tpu as pltpu

HEAD_DIM = 128
EPS = 1e-6
LOG2E = 1.4426950408889634
ALIBI_MAX_BIAS = 8.0
DIL_PAIRS = ((128, 1), (512, 4), (2048, 16))
DIL_BLOCK = 128
DIL_PERM = 4
MASK_BIAS = -1e30
V7X_VMEM_BYTES = 64 * 1024 * 1024
VMEM_LIMIT = V7X_VMEM_BYTES - 8 * 1024 * 1024
IN_PROJ_TM, IN_PROJ_TN = 1024, 1024
OUT_PROJ_TM, OUT_PROJ_TN = 1024, 1024
OUT_PROJ_VMEM_LIMIT = V7X_VMEM_BYTES - 2 * 1024 * 1024
NORM_ROWS = 512

F32 = jnp.float32
BF16 = jnp.bfloat16


def _params(*sem):
    return pltpu.CompilerParams(dimension_semantics=sem, vmem_limit_bytes=VMEM_LIMIT)


def _pick(n, pref):
    t = min(pref, n)
    while n % t:
        t //= 2
    return t


def _silu(x):
    return x / (1.0 + jnp.exp(-x))


def _dot_nt(a, b):
    return lax.dot_general(a, b, (((1,), (1,)), ((), ())), preferred_element_type=F32)


def _ada_kernel(c_ref, w_ref, b_ref, o_ref):
    cs = _silu(c_ref[...])
    acc = jnp.dot(cs.astype(BF16), w_ref[...].astype(BF16), preferred_element_type=F32)
    o_ref[...] = acc + b_ref[...]


def _ada(c_pad, w, b):
    m, d = c_pad.shape
    n = w.shape[1]
    tn = _pick(n, 512)
    return pl.pallas_call(
        _ada_kernel,
        grid=(n // tn,),
        in_specs=[
            pl.BlockSpec((m, d), lambda j: (0, 0)),
            pl.BlockSpec((d, tn), lambda j: (0, j)),
            pl.BlockSpec((1, tn), lambda j: (0, j)),
        ],
        out_specs=pl.BlockSpec((m, tn), lambda j: (0, j)),
        out_shape=jax.ShapeDtypeStruct((m, n), F32),
        compiler_params=_params("arbitrary"),
        name="ada_ln",
    )(c_pad, w, b)


def _norm_mod_kernel(x_ref, g_ref, scale_ref, shift_ref, h_ref, gs_ref):
    @pl.when(pl.program_id(1) == 0)
    def _():
        gs_ref[...] = g_ref[...] * (1.0 + scale_ref[...])

    x = x_ref[...]
    ms = jnp.mean(x * x, axis=-1, keepdims=True)
    h_ref[...] = (x * lax.rsqrt(ms + EPS) * gs_ref[...] + shift_ref[...]).astype(BF16)


def _norm_mod(x, g, scale, shift):
    b, s, d = x.shape
    ts = _pick(s, NORM_ROWS)
    return pl.pallas_call(
        _norm_mod_kernel,
        grid=(b, s // ts),
        in_specs=[
            pl.BlockSpec((None, ts, d), lambda i, j: (i, j, 0)),
            pl.BlockSpec((1, d), lambda i, j: (0, 0)),
            pl.BlockSpec((None, 1, d), lambda i, j: (i, 0, 0)),
            pl.BlockSpec((None, 1, d), lambda i, j: (i, 0, 0)),
        ],
        out_specs=pl.BlockSpec((None, ts, d), lambda i, j: (i, j, 0)),
        out_shape=jax.ShapeDtypeStruct((b, s, d), BF16),
        scratch_shapes=[pltpu.VMEM((1, d), F32)],
        compiler_params=_params("arbitrary", "arbitrary"),
        name="norm_mod",
    )(x, g, scale, shift)


def _inproj_kernel(h_ref, wk_ref, cs_ref, o_ref, wb_ref, *maybe_slabs, perm):
    j = pl.program_id(0)
    i = pl.program_id(1)
    kc = wk_ref.shape[0]

    def cast_chunk():
        rows = pl.ds(pl.multiple_of(i * kc, kc), kc)
        wb_ref[j % 2, rows, :] = (wk_ref[...] * cs_ref[...]).astype(BF16)

    @pl.when(j == 0)
    def _():
        cast_chunk()

    @pl.when(j > 0)
    def _():
        cast_chunk()
        vals = jnp.dot(h_ref[...], wb_ref[(j - 1) % 2], preferred_element_type=F32)
        if perm == 1:
            o_ref[...] = vals.astype(BF16)
        else:
            (slab_ref,) = maybe_slabs
            tm, tn = vals.shape
            lanes = slab_ref.shape[2]
            for slab in range(tn // lanes):
                slab_ref[slab] = vals[:, slab * lanes:(slab + 1) * lanes]
            for res in range(perm):
                for slab in range(tn // lanes):
                    rows = slab_ref[slab, pl.ds(res, tm // perm, stride=perm), :]
                    o_ref[res, :, slab * lanes:(slab + 1) * lanes] = rows.astype(BF16)


def _inproj(h, w, colscale, col0, ncols, perm):
    b, s, d = h.shape
    tm = _pick(s, IN_PROJ_TM)
    tn = _pick(ncols, IN_PROJ_TN)
    nm = s // tm
    steps = b * nm
    ncb = ncols // tn
    joff = col0 // tn
    kc = d // steps
    assert col0 % tn == 0 and tm % (8 * perm) == 0 and d % steps == 0 and kc % 16 == 0

    def row(j, i):
        i = jnp.where(j == 0, 0, i)
        return i // nm, i % nm

    def next_col(j):
        return joff + jnp.minimum(j, ncb - 1)

    in_specs = [
        pl.BlockSpec((None, tm, d), lambda j, i: (*row(j, i), 0)),
        pl.BlockSpec((kc, tn), lambda j, i: (i, next_col(j))),
        pl.BlockSpec((1, tn), lambda j, i: (0, next_col(j))),
    ]
    scratch = [pltpu.VMEM((2, d, tn), BF16)]
    if perm == 1:
        out_spec = pl.BlockSpec((None, tm, tn), lambda j, i: (*row(j, i), jnp.maximum(j - 1, 0)))
        out_shape = (b, s, ncols)
    else:
        out_spec = pl.BlockSpec((None, perm, tm // perm, tn),
                                lambda j, i: (row(j, i)[0], 0, row(j, i)[1], jnp.maximum(j - 1, 0)))
        out_shape = (b, perm, s // perm, ncols)
        scratch.append(pltpu.VMEM((tn // HEAD_DIM, tm, HEAD_DIM), F32))
    out = pl.pallas_call(
        functools.partial(_inproj_kernel, perm=perm),
        grid=(ncb + 1, steps),
        in_specs=in_specs,
        out_specs=out_spec,
        out_shape=jax.ShapeDtypeStruct(out_shape, BF16),
        scratch_shapes=scratch,
        compiler_params=_params("arbitrary", "arbitrary"),
        name="in_proj_p%d" % perm,
    )(h, w, colscale)
    return out.reshape(b, s, ncols)


def _head_epilogue(o, g, zg):
    ms = jnp.mean(o * o, axis=-1, keepdims=True)
    return o * lax.rsqrt(ms + EPS) * g * _silu(zg.astype(F32))


def _sb_kernel(q_ref, k_ref, v_ref, z_ref, g_ref, o_ref, acc_ref, carry_ref, *, seq, tile):
    row = lax.broadcasted_iota(jnp.int32, (tile, tile), 0)
    col = lax.broadcasted_iota(jnp.int32, (tile, tile), 1)
    causal = col < row
    upper = jnp.where(row > col, 1.0, 0.0).astype(BF16)

    def key_tile(q, k, v, carry, mask):
        s = _dot_nt(q, k)
        e = jnp.exp2(-jnp.abs(s))
        lb = jnp.minimum(s, 0.0) - jnp.log(1.0 + e) * LOG2E
        lom = lb - s
        if mask is not None:
            lom = jnp.where(mask, lom, 0.0)
        arg = lb + jnp.dot(lom.astype(BF16), upper, preferred_element_type=F32)
        if carry is not None:
            arg = arg + carry
        a = jnp.exp2(arg)
        if mask is not None:
            a = jnp.where(mask, a, 0.0)
        av = jnp.dot(a.astype(BF16), v, preferred_element_type=F32)
        return av, jnp.sum(lom, axis=1, keepdims=True)

    for kj in reversed(range(seq // tile)):
        k0 = kj * tile
        k = k_ref[pl.ds(k0, tile), :]
        v = v_ref[pl.ds(k0, tile), :]
        av, rsum = key_tile(q_ref[pl.ds(k0, tile), :], k, v, None, causal)
        acc_ref[pl.ds(k0, tile), :] = av
        carry_ref[pl.ds(k0, tile), :] = rsum
        if k0 + tile < seq:
            later = pl.ds(k0 + tile, seq - k0 - tile)
            av, rsum = key_tile(q_ref[later, :], k, v, carry_ref[later, :], None)
            acc_ref[later, :] += av
            carry_ref[later, :] += rsum

    for q0 in range(0, seq, tile):
        rows = pl.ds(q0, tile)
        o_ref[rows, :] = _head_epilogue(acc_ref[rows, :], g_ref[...], z_ref[rows, :]).astype(BF16)


def _head_spec(s, n_heads, off):
    return pl.BlockSpec((None, s, HEAD_DIM), lambda i, h: (i, 0, off * n_heads + h))


def _sb_attention(proj, g_sb, n_heads):
    b, s, _ = proj.shape
    tile = _pick(s, 256)
    return pl.pallas_call(
        functools.partial(_sb_kernel, seq=s, tile=tile),
        grid=(b, n_heads),
        in_specs=[_head_spec(s, n_heads, 0), _head_spec(s, n_heads, 1), _head_spec(s, n_heads, 2),
                  _head_spec(s, n_heads, 3), pl.BlockSpec((1, HEAD_DIM), lambda i, h: (0, h))],
        out_specs=pl.BlockSpec((None, s, HEAD_DIM), lambda i, h: (i, 0, h)),
        out_shape=jax.ShapeDtypeStruct((b, s, n_heads * HEAD_DIM), BF16),
        scratch_shapes=[pltpu.VMEM((s, HEAD_DIM), F32), pltpu.VMEM((s, 1), F32)],
        compiler_params=_params("arbitrary", "arbitrary"),
        name="sb_attn",
    )(proj, proj, proj, proj, g_sb)


def _dil_kernel(q_ref, k_ref, v_ref, z_ref, g_ref, slope_ref, o_ref,
                qf, kf, vf, q16, k16, vo, vo16, num_s, den_s, max_s, onat, *, seq):
    n = DIL_BLOCK
    seg = seq // DIL_PERM
    sub = n // DIL_PERM
    slope = slope_ref[0:1, 0:1]

    def bias_of(dist, window):
        return jnp.where((dist >= 0) & (dist <= window), dist.astype(F32) * (-slope), MASK_BIAS)

    def iota2(nk):
        return (lax.broadcasted_iota(jnp.int32, (n, nk), 0), lax.broadcasted_iota(jnp.int32, (n, nk), 1))

    def attend(q, k, vv, bias):
        s = _dot_nt(q, k) + bias
        m = jnp.max(s, axis=1, keepdims=True)
        p = jnp.exp2(s - m)
        pv = jnp.dot(p.astype(BF16), vv, preferred_element_type=F32)
        return pv[:, :HEAD_DIM], pv[:, HEAD_DIM:], jnp.broadcast_to(m, (n, HEAD_DIM))

    def put(br, idx, res):
        num_s[br, idx, :], den_s[br, idx, :], max_s[br, idx, :] = res

    ones = jnp.ones((seq, HEAD_DIM), BF16)
    vo[:, :HEAD_DIM] = v_ref[...]
    vo[:, HEAD_DIM:] = ones
    vo16[:, HEAD_DIM:] = ones

    a, c = iota2(2 * n)
    tq = (a % sub) * DIL_PERM + a // sub
    bias_prev = bias_of(tq - ((c % (2 * sub)) * DIL_PERM + c // (2 * sub) - n), DIL_PAIRS[0][0])
    a, c = iota2(n)
    bias_first = bias_of(tq[:, :n] - ((c % sub) * DIL_PERM + c // sub), DIL_PAIRS[0][0])

    def gather(ref, start, size):
        return jnp.concatenate([ref[pl.ds(res * seg + start, size), :] for res in range(DIL_PERM)], axis=0)

    for i in range(seq // n):
        q = gather(q_ref, sub * i, sub)
        if i == 0:
            res3 = attend(q, gather(k_ref, 0, sub), gather(vo, 0, sub), bias_first)
        else:
            res3 = attend(q, gather(k_ref, sub * (i - 1), 2 * sub), gather(vo, sub * (i - 1), 2 * sub), bias_prev)
        for res in range(DIL_PERM):
            put(0, pl.ds(res * seg + sub * i, sub), tuple(t[res * sub:(res + 1) * sub] for t in res3))

    r = DIL_PAIRS[1][1]
    assert r == DIL_PERM
    a, c = iota2(2 * n)
    bias_prev = bias_of(r * (a - c + n), DIL_PAIRS[1][0])
    bias_first = bias_prev[:, n:]
    r16 = DIL_PAIRS[2][1]
    step = r16 // DIL_PERM
    assert seq // r16 == n
    a, c = iota2(n)
    bias_16 = bias_of(r16 * (a - c), DIL_PAIRS[2][0])
    chunk = _pick(seg, 256)

    for res in range(DIL_PERM):
        grp = pl.ds(res * seg, seg)
        for i in range(seg // n):
            row0 = res * seg + i * n
            q = q_ref[pl.ds(row0, n), :]
            if i == 0:
                res3 = attend(q, k_ref[pl.ds(row0, n), :], vo[pl.ds(row0, n), :], bias_first)
            else:
                res3 = attend(q, k_ref[pl.ds(row0 - n, 2 * n), :], vo[pl.ds(row0 - n, 2 * n), :], bias_prev)
            put(1, pl.ds(row0, n), res3)

        qf[grp, :] = q_ref[grp, :].astype(F32)
        kf[grp, :] = k_ref[grp, :].astype(F32)
        vf[grp, :] = v_ref[grp, :].astype(F32)
        for sres in range(step):
            src = pl.ds(res * seg + sres, n, stride=step)
            blk = pl.ds((res * step + sres) * n, n)
            q16[blk, :] = qf[src, :].astype(BF16)
            k16[blk, :] = kf[src, :].astype(BF16)
            vo16[blk, :HEAD_DIM] = vf[src, :].astype(BF16)
            put(2, src, attend(q16[blk, :], k16[blk, :], vo16[blk, :], bias_16))

        for l0 in range(0, seg, chunk):
            rows = pl.ds(res * seg + l0, chunk)
            m_all = jnp.maximum(jnp.maximum(max_s[0, rows, :], max_s[1, rows, :]), max_s[2, rows, :])
            num = jnp.zeros((chunk, HEAD_DIM), F32)
            den = jnp.zeros((chunk, HEAD_DIM), F32)
            for br in range(len(DIL_PAIRS)):
                w = jnp.exp2(max_s[br, rows, :] - m_all)
                num = num + num_s[br, rows, :] * w
                den = den + den_s[br, rows, :] * w
            onat[pl.ds(DIL_PERM * l0 + res, chunk, stride=DIL_PERM), :] = _head_epilogue(
                num / den, g_ref[...], z_ref[rows, :])
    for c0 in range(0, seq, chunk):
        o_ref[pl.ds(c0, chunk), :] = onat[pl.ds(c0, chunk), :].astype(BF16)


def _dil_attention(proj, g_dil, slopes, n_heads):
    b, s, _ = proj.shape
    nbr = len(DIL_PAIRS)
    rows_f32 = pltpu.VMEM((s, HEAD_DIM), F32)
    rows_bf16 = pltpu.VMEM((s, HEAD_DIM), BF16)
    rows2_bf16 = pltpu.VMEM((s, 2 * HEAD_DIM), BF16)
    per_branch = pltpu.VMEM((nbr, s, HEAD_DIM), F32)
    return pl.pallas_call(
        functools.partial(_dil_kernel, seq=s),
        grid=(b, n_heads),
        in_specs=[_head_spec(s, n_heads, 0), _head_spec(s, n_heads, 1), _head_spec(s, n_heads, 2),
                  _head_spec(s, n_heads, 3), pl.BlockSpec((1, HEAD_DIM), lambda i, h: (0, h)),
                  pl.BlockSpec((None, 1, HEAD_DIM), lambda i, h: (h, 0, 0))],
        out_specs=pl.BlockSpec((None, s, HEAD_DIM), lambda i, h: (i, 0, h)),
        out_shape=jax.ShapeDtypeStruct((b, s, n_heads * HEAD_DIM), BF16),
        scratch_shapes=[rows_f32, rows_f32, rows_f32, rows_bf16, rows_bf16, rows2_bf16, rows2_bf16,
                        per_branch, per_branch, per_branch, rows_f32],
        compiler_params=_params("arbitrary", "arbitrary"),
        name="dil_attn",
    )(proj, proj, proj, proj, g_dil, slopes)


def _outproj_kernel(ya_ref, yb_ref, wk_ref, x_ref, gate_ref, o_ref, wb_ref):
    j = pl.program_id(0)
    i = pl.program_id(1)
    kc = wk_ref.shape[0]
    ka = ya_ref.shape[1]

    def cast_chunk():
        wb_ref[j % 2, pl.ds(pl.multiple_of(i * kc, kc), kc), :] = wk_ref[...].astype(BF16)

    @pl.when(j == 0)
    def _():
        cast_chunk()

    @pl.when(j > 0)
    def _():
        cast_chunk()
        cur = (j - 1) % 2
        acc = jnp.dot(ya_ref[...], wb_ref[cur, :ka, :], preferred_element_type=F32)
        acc = acc + jnp.dot(yb_ref[...], wb_ref[cur, ka:, :], preferred_element_type=F32)
        o_ref[...] = x_ref[...] + gate_ref[...] * acc


def _outproj(y_sb, y_dl, w_out, x, gate):
    b, s, d = x.shape
    ka = y_sb.shape[2]
    kb = y_dl.shape[2]
    tm = _pick(s, OUT_PROJ_TM)
    tn = _pick(d, OUT_PROJ_TN)
    nm = s // tm
    steps = b * nm
    ncb = d // tn
    kc = (ka + kb) // steps
    assert (ka + kb) % steps == 0 and kc % 16 == 0

    def row(j, i):
        i = jnp.where(j == 0, 0, i)
        return i // nm, i % nm

    def col(j):
        return jnp.maximum(j - 1, 0)

    return pl.pallas_call(
        _outproj_kernel,
        grid=(ncb + 1, steps),
        in_specs=[
            pl.BlockSpec((None, tm, ka), lambda j, i: (*row(j, i), 0)),
            pl.BlockSpec((None, tm, kb), lambda j, i: (*row(j, i), 0)),
            pl.BlockSpec((kc, tn), lambda j, i: (i, jnp.minimum(j, ncb - 1))),
            pl.BlockSpec((None, tm, tn), lambda j, i: (*row(j, i), col(j))),
            pl.BlockSpec((None, 1, tn), lambda j, i: (row(j, i)[0], 0, col(j))),
        ],
        out_specs=pl.BlockSpec((None, tm, tn), lambda j, i: (*row(j, i), col(j))),
        out_shape=jax.ShapeDtypeStruct((b, s, d), F32),
        scratch_shapes=[pltpu.VMEM((2, ka + kb, tn), BF16)],
        compiler_params=pltpu.CompilerParams(dimension_semantics=("arbitrary", "arbitrary"),
                                             vmem_limit_bytes=OUT_PROJ_VMEM_LIMIT),
        name="out_proj",
    )(y_sb, y_dl, w_out, x, gate)


def _final_norm_kernel(x_ref, g_ref, o_ref):
    x = x_ref[...]
    ms = jnp.mean(x * x, axis=-1, keepdims=True)
    o_ref[...] = x * lax.rsqrt(ms + EPS) * g_ref[...]


def _final_norm(x, g):
    b, s, d = x.shape
    ts = _pick(s, NORM_ROWS)
    return pl.pallas_call(
        _final_norm_kernel,
        grid=(b, s // ts),
        in_specs=[pl.BlockSpec((None, ts, d), lambda i, j: (i, j, 0)),
                  pl.BlockSpec((1, d), lambda i, j: (0, 0))],
        out_specs=pl.BlockSpec((None, ts, d), lambda i, j: (i, j, 0)),
        out_shape=jax.ShapeDtypeStruct((b, s, d), F32),
        compiler_params=_params("arbitrary", "arbitrary"),
        name="final_norm",
    )(x, g)


def kernel(x, c, w_ada, b_ada, g_norm, w_in, g_sb, g_dil, w_out, g_final):
    b, s, d = x.shape
    depth = w_ada.shape[0]
    n_sb = g_sb.shape[1] // HEAD_DIM
    n_dil = g_dil.shape[1] // HEAD_DIM
    sb_w = n_sb * HEAD_DIM
    dil_w = n_dil * HEAD_DIM
    assert s % DIL_PAIRS[-1][0] == 0 and sb_w == dil_w

    qscale = LOG2E / math.sqrt(HEAD_DIM)
    colscale = jnp.concatenate([
        jnp.full((sb_w,), qscale, F32), jnp.ones((3 * sb_w,), F32),
        jnp.full((dil_w,), qscale, F32), jnp.ones((3 * dil_w,), F32)])[None, :]
    slopes = jnp.exp2(-ALIBI_MAX_BIAS * jnp.arange(1, n_dil + 1, dtype=F32) / n_dil) * LOG2E
    slopes = jnp.broadcast_to(slopes[:, None, None], (n_dil, 1, HEAD_DIM))

    pad = 16
    c_pad = jnp.zeros((pad, d), F32).at[:b].set(c.astype(F32))
    for layer in range(depth):
        mod = _ada(c_pad, w_ada[layer], b_ada[layer][None, :])[:b]
        shift, scale, gate = (mod[:, i * d:(i + 1) * d][:, None, :] for i in range(3))
        h = _norm_mod(x, g_norm[layer][None, :], scale, shift)
        proj_sb = _inproj(h, w_in[layer], colscale, 0, 4 * sb_w, 1)
        proj_dl = _inproj(h, w_in[layer], colscale, 4 * sb_w, 4 * dil_w, DIL_PERM)
        y_sb = _sb_attention(proj_sb, g_sb[layer][None, :], n_sb)
        y_dl = _dil_attention(proj_dl, g_dil[layer][None, :], slopes, n_dil)
        x = _outproj(y_sb, y_dl, w_out[layer], x, gate)
    return _final_norm(x, g_final[None, :])
```

```python
import functools
import math

import jax
import jax.numpy as jnp
from jax import lax
from jax.experimental import pallas as pl
from jax.experimental.pallas import tpu as pltpu

HEAD_DIM = 128
EPS = 1e-6
LOG2E = 1.4426950408889634
ALIBI_MAX_BIAS = 8.0
DIL_PAIRS = ((128, 1), (512, 4), (2048, 16))
DIL_BLOCK = 128
DIL_PERM = 4
MASK_BIAS = -1e30
V7X_VMEM_BYTES = 64 * 1024 * 1024
VMEM_LIMIT = V7X_VMEM_BYTES - 8 * 1024 * 1024
IN_PROJ_TM, IN_PROJ_TN = 1024, 1024
OUT_PROJ_TM, OUT_PROJ_TN = 1024, 1024
OUT_PROJ_VMEM_LIMIT = V7X_VMEM_BYTES - 2 * 1024 * 1024
NORM_ROWS = 512

F32 = jnp.float32
BF16 = jnp.bfloat16


def _params(*sem):
    return pltpu.CompilerParams(dimension_semantics=sem, vmem_limit_bytes=VMEM_LIMIT)


def _pick(n, pref):
    t = min(pref, n)
    while n % t:
        t //= 2
    return t


def _silu(x):
    return x / (1.0 + jnp.exp(-x))


def _dot_nt(a, b):
    return lax.dot_general(a, b, (((1,), (1,)), ((), ())), preferred_element_type=F32)


def _ada_kernel(c_ref, w_ref, b_ref, o_ref):
    cs = _silu(c_ref[...])
    acc = jnp.dot(cs.astype(BF16), w_ref[...].astype(BF16), preferred_element_type=F32)
    o_ref[...] = acc + b_ref[...]


def _ada(c_pad, w, b):
    m, d = c_pad.shape
    n = w.shape[1]
    tn = _pick(n, 512)
    return pl.pallas_call(
        _ada_kernel,
        grid=(n // tn,),
        in_specs=[
            pl.BlockSpec((m, d), lambda j: (0, 0)),
            pl.BlockSpec((d, tn), lambda j: (0, j)),
            pl.BlockSpec((1, tn), lambda j: (0, j)),
        ],
        out_specs=pl.BlockSpec((m, tn), lambda j: (0, j)),
        out_shape=jax.ShapeDtypeStruct((m, n), F32),
        compiler_params=_params("arbitrary"),
        name="ada_ln",
    )(c_pad, w, b)


def _norm_mod_kernel(x_ref, g_ref, scale_ref, shift_ref, h_ref, gs_ref):
    @pl.when(pl.program_id(1) == 0)
    def _():
        gs_ref[...] = g_ref[...] * (1.0 + scale_ref[...])

    x = x_ref[...]
    ms = jnp.mean(x * x, axis=-1, keepdims=True)
    h_ref[...] = (x * lax.rsqrt(ms + EPS) * gs_ref[...] + shift_ref[...]).astype(BF16)


def _norm_mod(x, g, scale, shift):
    b, s, d = x.shape
    ts = _pick(s, NORM_ROWS)
    return pl.pallas_call(
        _norm_mod_kernel,
        grid=(b, s // ts),
        in_specs=[
            pl.BlockSpec((None, ts, d), lambda i, j: (i, j, 0)),
            pl.BlockSpec((1, d), lambda i, j: (0, 0)),
            pl.BlockSpec((None, 1, d), lambda i, j: (i, 0, 0)),
            pl.BlockSpec((None, 1, d), lambda i, j: (i, 0, 0)),
        ],
        out_specs=pl.BlockSpec((None, ts, d), lambda i, j: (i, j, 0)),
        out_shape=jax.ShapeDtypeStruct((b, s, d), BF16),
        scratch_shapes=[pltpu.VMEM((1, d), F32)],
        compiler_params=_params("arbitrary", "arbitrary"),
        name="norm_mod",
    )(x, g, scale, shift)


def _inproj_kernel(h_ref, wk_ref, cs_ref, o_ref, wb_ref, *maybe_slabs, perm):
    j = pl.program_id(0)
    i = pl.program_id(1)
    kc = wk_ref.shape[0]

    def cast_chunk():
        rows = pl.ds(pl.multiple_of(i * kc, kc), kc)
        wb_ref[j % 2, rows, :] = (wk_ref[...] * cs_ref[...]).astype(BF16)

    @pl.when(j == 0)
    def _():
        cast_chunk()

    @pl.when(j > 0)
    def _():
        cast_chunk()
        vals = jnp.dot(h_ref[...], wb_ref[(j - 1) % 2], preferred_element_type=F32)
        if perm == 1:
            o_ref[...] = vals.astype(BF16)
        else:
            (slab_ref,) = maybe_slabs
            tm, tn = vals.shape
            lanes = slab_ref.shape[2]
            for slab in range(tn // lanes):
                slab_ref[slab] = vals[:, slab * lanes:(slab + 1) * lanes]
            for res in range(perm):
                for slab in range(tn // lanes):
                    rows = slab_ref[slab, pl.ds(res, tm // perm, stride=perm), :]
                    o_ref[res, :, slab * lanes:(slab + 1) * lanes] = rows.astype(BF16)


def _inproj(h, w, colscale, col0, ncols, perm):
    b, s, d = h.shape
    tm = _pick(s, IN_PROJ_TM)
    tn = _pick(ncols, IN_PROJ_TN)
    nm = s // tm
    steps = b * nm
    ncb = ncols // tn
    joff = col0 // tn
    kc = d // steps
    assert col0 % tn == 0 and tm % (8 * perm) == 0 and d % steps == 0 and kc % 16 == 0

    def row(j, i):
        i = jnp.where(j == 0, 0, i)
        return i // nm, i % nm

    def next_col(j):
        return joff + jnp.minimum(j, ncb - 1)

    in_specs = [
        pl.BlockSpec((None, tm, d), lambda j, i: (*row(j, i), 0)),
        pl.BlockSpec((kc, tn), lambda j, i: (i, next_col(j))),
        pl.BlockSpec((1, tn), lambda j, i: (0, next_col(j))),
    ]
    scratch = [pltpu.VMEM((2, d, tn), BF16)]
    if perm == 1:
        out_spec = pl.BlockSpec((None, tm, tn), lambda j, i: (*row(j, i), jnp.maximum(j - 1, 0)))
        out_shape = (b, s, ncols)
    else:
        out_spec = pl.BlockSpec((None, perm, tm // perm, tn),
                                lambda j, i: (row(j, i)[0], 0, row(j, i)[1], jnp.maximum(j - 1, 0)))
        out_shape = (b, perm, s // perm, ncols)
        scratch.append(pltpu.VMEM((tn // HEAD_DIM, tm, HEAD_DIM), F32))
    out = pl.pallas_call(
        functools.partial(_inproj_kernel, perm=perm),
        grid=(ncb + 1, steps),
        in_specs=in_specs,
        out_specs=out_spec,
        out_shape=jax.ShapeDtypeStruct(out_shape, BF16),
        scratch_shapes=scratch,
        compiler_params=_params("arbitrary", "arbitrary"),
        name="in_proj_p%d" % perm,
    )(h, w, colscale)
    return out.reshape(b, s, ncols)


def _head_epilogue(o, g, zg):
    ms = jnp.mean(o * o, axis=-1, keepdims=True)
    return o * lax.rsqrt(ms + EPS) * g * _silu(zg.astype(F32))


def _sb_kernel(q_ref, k_ref, v_ref, z_ref, g_ref, o_ref, acc_ref, carry_ref, *, seq, tile):
    row = lax.broadcasted_iota(jnp.int32, (tile, tile), 0)
    col = lax.broadcasted_iota(jnp.int32, (tile, tile), 1)
    causal = col < row
    upper = jnp.where(row > col, 1.0, 0.0).astype(BF16)

    def key_tile(q, k, v, carry, mask):
        s = _dot_nt(q, k)
        e = jnp.exp2(-jnp.abs(s))
        lb = jnp.minimum(s, 0.0) - jnp.log(1.0 + e) * LOG2E
        lom = lb - s
        if mask is not None:
            lom = jnp.where(mask, lom, 0.0)
        lom_b = lom.astype(BF16)
        suffix = jnp.dot(lom_b, upper, preferred_element_type=F32)
        arg = lb + suffix
        if carry is not None:
            arg = arg + carry
        a = jnp.exp2(arg)
        if mask is not None:
            a = jnp.where(mask, a, 0.0)
        av = jnp.dot(a.astype(BF16), v, preferred_element_type=F32)
        return av, suffix[:, :1] + lom_b[:, :1].astype(F32)

    for kj in reversed(range(seq // tile)):
        k0 = kj * tile
        k = k_ref[pl.ds(k0, tile), :]
        v = v_ref[pl.ds(k0, tile), :]
        av, rsum = key_tile(q_ref[pl.ds(k0, tile), :], k, v, None, causal)
        acc_ref[pl.ds(k0, tile), :] = av
        carry_ref[pl.ds(k0, tile), :] = rsum
        if k0 + tile < seq:
            later = pl.ds(k0 + tile, seq - k0 - tile)
            av, rsum = key_tile(q_ref[later, :], k, v, carry_ref[later, :], None)
            acc_ref[later, :] += av
            carry_ref[later, :] += rsum

    for q0 in range(0, seq, tile):
        rows = pl.ds(q0, tile)
        o_ref[rows, :] = _head_epilogue(acc_ref[rows, :], g_ref[...], z_ref[rows, :]).astype(BF16)


def _head_spec(s, n_heads, off):
    return pl.BlockSpec((None, s, HEAD_DIM), lambda i, h: (i, 0, off * n_heads + h))


def _sb_attention(proj, g_sb, n_heads):
    b, s, _ = proj.shape
    tile = _pick(s, 256)
    return pl.pallas_call(
        functools.partial(_sb_kernel, seq=s, tile=tile),
        grid=(b, n_heads),
        in_specs=[_head_spec(s, n_heads, 0), _head_spec(s, n_heads, 1), _head_spec(s, n_heads, 2),
                  _head_spec(s, n_heads, 3), pl.BlockSpec((1, HEAD_DIM), lambda i, h: (0, h))],
        out_specs=pl.BlockSpec((None, s, HEAD_DIM), lambda i, h: (i, 0, h)),
        out_shape=jax.ShapeDtypeStruct((b, s, n_heads * HEAD_DIM), BF16),
        scratch_shapes=[pltpu.VMEM((s, HEAD_DIM), F32), pltpu.VMEM((s, 1), F32)],
        compiler_params=_params("arbitrary", "arbitrary"),
        name="sb_attn",
    )(proj, proj, proj, proj, g_sb)


def _dil_kernel(q_ref, k_ref, v_ref, z_ref, g_ref, slope_ref, o_ref,
                qf, kf, vf, q16, k16, vo, vo16, num_s, den_s, max_s, onat, *, seq):
    n = DIL_BLOCK
    seg = seq // DIL_PERM
    sub = n // DIL_PERM
    slope = slope_ref[0:1, 0:1]

    def bias_of(dist, window):
        return jnp.where((dist >= 0) & (dist <= window), dist.astype(F32) * (-slope), MASK_BIAS)

    def iota2(nk):
        return (lax.broadcasted_iota(jnp.int32, (n, nk), 0), lax.broadcasted_iota(jnp.int32, (n, nk), 1))

    def attend(q, k, vv, bias):
        s = _dot_nt(q, k) + bias
        m = jnp.max(s, axis=1, keepdims=True)
        p = jnp.exp2(s - m)
        pv = jnp.dot(p.astype(BF16), vv, preferred_element_type=F32)
        return pv[:, :HEAD_DIM], pv[:, HEAD_DIM:], jnp.broadcast_to(m, (n, HEAD_DIM))

    def put(br, idx, res):
        num_s[br, idx, :], den_s[br, idx, :], max_s[br, idx, :] = res

    ones = jnp.ones((seq, HEAD_DIM), BF16)
    vo[:, :HEAD_DIM] = v_ref[...]
    vo[:, HEAD_DIM:] = ones
    vo16[:, HEAD_DIM:] = ones

    a, c = iota2(2 * n)
    tq = (a % sub) * DIL_PERM + a // sub
    bias_prev = bias_of(tq - ((c % (2 * sub)) * DIL_PERM + c // (2 * sub) - n), DIL_PAIRS[0][0])
    a, c = iota2(n)
    bias_first = bias_of(tq[:, :n] - ((c % sub) * DIL_PERM + c // sub), DIL_PAIRS[0][0])

    def gather(ref, start, size):
        return jnp.concatenate([ref[pl.ds(res * seg + start, size), :] for res in range(DIL_PERM)], axis=0)

    for i in range(seq // n):
        q = gather(q_ref, sub * i, sub)
        if i == 0:
            res3 = attend(q, gather(k_ref, 0, sub), gather(vo, 0, sub), bias_first)
        else:
            res3 = attend(q, gather(k_ref, sub * (i - 1), 2 * sub), gather(vo, sub * (i - 1), 2 * sub), bias_prev)
        for res in range(DIL_PERM):
            put(0, pl.ds(res * seg + sub * i, sub), tuple(t[res * sub:(res + 1) * sub] for t in res3))

    r = DIL_PAIRS[1][1]
    assert r == DIL_PERM
    a, c = iota2(2 * n)
    bias_prev = bias_of(r * (a - c + n), DIL_PAIRS[1][0])
    bias_first = bias_prev[:, n:]
    r16 = DIL_PAIRS[2][1]
    step = r16 // DIL_PERM
    assert seq // r16 == n
    a, c = iota2(n)
    bias_16 = bias_of(r16 * (a - c), DIL_PAIRS[2][0])
    chunk = _pick(seg, 256)

    for res in range(DIL_PERM):
        grp = pl.ds(res * seg, seg)
        for i in range(seg // n):
            row0 = res * seg + i * n
            q = q_ref[pl.ds(row0, n), :]
            if i == 0:
                res3 = attend(q, k_ref[pl.ds(row0, n), :], vo[pl.ds(row0, n), :], bias_first)
            else:
                res3 = attend(q, k_ref[pl.ds(row0 - n, 2 * n), :], vo[pl.ds(row0 - n, 2 * n), :], bias_prev)
            put(1, pl.ds(row0, n), res3)

        qf[grp, :] = q_ref[grp, :].astype(F32)
        kf[grp, :] = k_ref[grp, :].astype(F32)
        vf[grp, :] = v_ref[grp, :].astype(F32)
        for sres in range(step):
            src = pl.ds(res * seg + sres, n, stride=step)
            blk = pl.ds((res * step + sres) * n, n)
            q16[blk, :] = qf[src, :].astype(BF16)
            k16[blk, :] = kf[src, :].astype(BF16)
            vo16[blk, :HEAD_DIM] = vf[src, :].astype(BF16)
            put(2, src, attend(q16[blk, :], k16[blk, :], vo16[blk, :], bias_16))

        for l0 in range(0, seg, chunk):
            rows = pl.ds(res * seg + l0, chunk)
            m_all = jnp.maximum(jnp.maximum(max_s[0, rows, :], max_s[1, rows, :]), max_s[2, rows, :])
            num = jnp.zeros((chunk, HEAD_DIM), F32)
            den = jnp.zeros((chunk, HEAD_DIM), F32)
            for br in range(len(DIL_PAIRS)):
                w = jnp.exp2(max_s[br, rows, :] - m_all)
                num = num + num_s[br, rows, :] * w
                den = den + den_s[br, rows, :] * w
            onat[pl.ds(DIL_PERM * l0 + res, chunk, stride=DIL_PERM), :] = _head_epilogue(
                num / den, g_ref[...], z_ref[rows, :])
    for c0 in range(0, seq, chunk):
        o_ref[pl.ds(c0, chunk), :] = onat[pl.ds(c0, chunk), :].astype(BF16)


def _dil_attention(proj, g_dil, slopes, n_heads):
    b, s, _ = proj.shape
    nbr = len(DIL_PAIRS)
    rows_f32 = pltpu.VMEM((s, HEAD_DIM), F32)
    rows_bf16 = pltpu.VMEM((s, HEAD_DIM), BF16)
    rows2_bf16 = pltpu.VMEM((s, 2 * HEAD_DIM), BF16)
    per_branch = pltpu.VMEM((nbr, s, HEAD_DIM), F32)
    return pl.pallas_call(
        functools.partial(_dil_kernel, seq=s),
        grid=(b, n_heads),
        in_specs=[_head_spec(s, n_heads, 0), _head_spec(s, n_heads, 1), _head_spec(s, n_heads, 2),
                  _head_spec(s, n_heads, 3), pl.BlockSpec((1, HEAD_DIM), lambda i, h: (0, h)),
                  pl.BlockSpec((None, 1, HEAD_DIM), lambda i, h: (h, 0, 0))],
        out_specs=pl.BlockSpec((None, s, HEAD_DIM), lambda i, h: (i, 0, h)),
        out_shape=jax.ShapeDtypeStruct((b, s, n_heads * HEAD_DIM), BF16),
        scratch_shapes=[rows_f32, rows_f32, rows_f32, rows_bf16, rows_bf16, rows2_bf16, rows2_bf16,
                        per_branch, per_branch, per_branch, rows_f32],
        compiler_params=_params("arbitrary", "arbitrary"),
        name="dil_attn",
    )(proj, proj, proj, proj, g_dil, slopes)


def _outproj_kernel(ya_ref, yb_ref, wk_ref, x_ref, gate_ref, o_ref, wb_ref):
    j = pl.program_id(0)
    i = pl.program_id(1)
    kc = wk_ref.shape[0]
    ka = ya_ref.shape[1]

    def cast_chunk():
        wb_ref[j % 2, pl.ds(pl.multiple_of(i * kc, kc), kc), :] = wk_ref[...].astype(BF16)

    @pl.when(j == 0)
    def _():
        cast_chunk()

    @pl.when(j > 0)
    def _():
        cast_chunk()
        cur = (j - 1) % 2
        acc = jnp.dot(ya_ref[...], wb_ref[cur, :ka, :], preferred_element_type=F32)
        acc = acc + jnp.dot(yb_ref[...], wb_ref[cur, ka:, :], preferred_element_type=F32)
        o_ref[...] = x_ref[...] + gate_ref[...] * acc


def _outproj(y_sb, y_dl, w_out, x, gate):
    b, s, d = x.shape
    ka = y_sb.shape[2]
    kb = y_dl.shape[2]
    tm = _pick(s, OUT_PROJ_TM)
    tn = _pick(d, OUT_PROJ_TN)
    nm = s // tm
    steps = b * nm
    ncb = d // tn
    kc = (ka + kb) // steps
    assert (ka + kb) % steps == 0 and kc % 16 == 0

    def row(j, i):
        i = jnp.where(j == 0, 0, i)
        return i // nm, i % nm

    def col(j):
        return jnp.maximum(j - 1, 0)

    return pl.pallas_call(
        _outproj_kernel,
        grid=(ncb + 1, steps),
        in_specs=[
            pl.BlockSpec((None, tm, ka), lambda j, i: (*row(j, i), 0)),
            pl.BlockSpec((None, tm, kb), lambda j, i: (*row(j, i), 0)),
            pl.BlockSpec((kc, tn), lambda j, i: (i, jnp.minimum(j, ncb - 1))),
            pl.BlockSpec((None, tm, tn), lambda j, i: (*row(j, i), col(j))),
            pl.BlockSpec((None, 1, tn), lambda j, i: (row(j, i)[0], 0, col(j))),
        ],
        out_specs=pl.BlockSpec((None, tm, tn), lambda j, i: (*row(j, i), col(j))),
        out_shape=jax.ShapeDtypeStruct((b, s, d), F32),
        scratch_shapes=[pltpu.VMEM((2, ka + kb, tn), BF16)],
        compiler_params=pltpu.CompilerParams(dimension_semantics=("arbitrary", "arbitrary"),
                                             vmem_limit_bytes=OUT_PROJ_VMEM_LIMIT),
        name="out_proj",
    )(y_sb, y_dl, w_out, x, gate)


def _final_norm_kernel(x_ref, g_ref, o_ref):
    x = x_ref[...]
    ms = jnp.mean(x * x, axis=-1, keepdims=True)
    o_ref[...] = x * lax.rsqrt(ms + EPS) * g_ref[...]


def _final_norm(x, g):
    b, s, d = x.shape
    ts = _pick(s, NORM_ROWS)
    return pl.pallas_call(
        _final_norm_kernel,
        grid=(b, s // ts),
        in_specs=[pl.BlockSpec((None, ts, d), lambda i, j: (i, j, 0)),
                  pl.BlockSpec((1, d), lambda i, j: (0, 0))],
        out_specs=pl.BlockSpec((None, ts, d), lambda i, j: (i, j, 0)),
        out_shape=jax.ShapeDtypeStruct((b, s, d), F32),
        compiler_params=_params("arbitrary", "arbitrary"),
        name="final_norm",
    )(x, g)


def kernel(x, c, w_ada, b_ada, g_norm, w_in, g_sb, g_dil, w_out, g_final):
    b, s, d = x.shape
    depth = w_ada.shape[0]
    n_sb = g_sb.shape[1] // HEAD_DIM
    n_dil = g_dil.shape[1] // HEAD_DIM
    sb_w = n_sb * HEAD_DIM
    dil_w = n_dil * HEAD_DIM
    assert s % DIL_PAIRS[-1][0] == 0 and sb_w == dil_w

    qscale = LOG2E / math.sqrt(HEAD_DIM)
    colscale = jnp.concatenate([
        jnp.full((sb_w,), qscale, F32), jnp.ones((3 * sb_w,), F32),
        jnp.full((dil_w,), qscale, F32), jnp.ones((3 * dil_w,), F32)])[None, :]
    slopes = jnp.exp2(-ALIBI_MAX_BIAS * jnp.arange(1, n_dil + 1, dtype=F32) / n_dil) * LOG2E
    slopes = jnp.broadcast_to(slopes[:, None, None], (n_dil, 1, HEAD_DIM))

    pad = 16
    c_pad = jnp.zeros((pad, d), F32).at[:b].set(c.astype(F32))
    for layer in range(depth):
        mod = _ada(c_pad, w_ada[layer], b_ada[layer][None, :])[:b]
        shift, scale, gate = (mod[:, i * d:(i + 1) * d][:, None, :] for i in range(3))
        h = _norm_mod(x, g_norm[layer][None, :], scale, shift)
        proj_sb = _inproj(h, w_in[layer], colscale, 0, 4 * sb_w, 1)
        proj_dl = _inproj(h, w_in[layer], colscale, 4 * sb_w, 4 * dil_w, DIL_PERM)
        y_sb = _sb_attention(proj_sb, g_sb[layer][None, :], n_sb)
        y_dl = _dil_attention(proj_dl, g_dil[layer][None, :], slopes, n_dil)
        x = _outproj(y_sb, y_dl, w_out[layer], x, gate)
    return _final_norm(x, g_final[None, :])
```

```python
import functools
import math

import jax
import jax.numpy as jnp
from jax import lax
from jax.experimental import pallas as pl
from jax.experimental.pallas import tpu as pltpu

HEAD_DIM = 128
EPS = 1e-6
LOG2E = 1.4426950408889634
ALIBI_MAX_BIAS = 8.0
DIL_PAIRS = ((128, 1), (512, 4), (2048, 16))
DIL_BLOCK = 128
DIL_PERM = 4
MASK_BIAS = -1e30
V7X_VMEM_BYTES = 64 * 1024 * 1024
VMEM_LIMIT = V7X_VMEM_BYTES - 8 * 1024 * 1024
IN_PROJ_TM, IN_PROJ_TN = 1024, 1024
OUT_PROJ_TM, OUT_PROJ_TN = 1024, 1024
OUT_PROJ_VMEM_LIMIT = V7X_VMEM_BYTES - 2 * 1024 * 1024
NORM_ROWS = 512
DIL_HEADS_PER_STEP = 2

F32 = jnp.float32
BF16 = jnp.bfloat16


def _params(*sem):
    return pltpu.CompilerParams(dimension_semantics=sem, vmem_limit_bytes=VMEM_LIMIT)


def _pick(n, pref):
    t = min(pref, n)
    while n % t:
        t //= 2
    return t


def _silu(x):
    return x / (1.0 + jnp.exp(-x))


def _dot_nt(a, b):
    return lax.dot_general(a, b, (((1,), (1,)), ((), ())), preferred_element_type=F32)


def _ada_kernel(c_ref, w_ref, b_ref, o_ref):
    cs = _silu(c_ref[...])
    acc = jnp.dot(cs.astype(BF16), w_ref[...].astype(BF16), preferred_element_type=F32)
    o_ref[...] = acc + b_ref[...]


def _ada(c_pad, w, b):
    m, d = c_pad.shape
    n = w.shape[1]
    tn = _pick(n, 512)
    return pl.pallas_call(
        _ada_kernel,
        grid=(n // tn,),
        in_specs=[
            pl.BlockSpec((m, d), lambda j: (0, 0)),
            pl.BlockSpec((d, tn), lambda j: (0, j)),
            pl.BlockSpec((1, tn), lambda j: (0, j)),
        ],
        out_specs=pl.BlockSpec((m, tn), lambda j: (0, j)),
        out_shape=jax.ShapeDtypeStruct((m, n), F32),
        compiler_params=_params("arbitrary"),
        name="ada_ln",
    )(c_pad, w, b)


def _norm_mod_kernel(x_ref, g_ref, scale_ref, shift_ref, h_ref, gs_ref):
    @pl.when(pl.program_id(1) == 0)
    def _():
        gs_ref[...] = g_ref[...] * (1.0 + scale_ref[...])

    x = x_ref[...]
    ms = jnp.mean(x * x, axis=-1, keepdims=True)
    h_ref[...] = (x * lax.rsqrt(ms + EPS) * gs_ref[...] + shift_ref[...]).astype(BF16)


def _norm_mod(x, g, scale, shift):
    b, s, d = x.shape
    ts = _pick(s, NORM_ROWS)
    return pl.pallas_call(
        _norm_mod_kernel,
        grid=(b, s // ts),
        in_specs=[
            pl.BlockSpec((None, ts, d), lambda i, j: (i, j, 0)),
            pl.BlockSpec((1, d), lambda i, j: (0, 0)),
            pl.BlockSpec((None, 1, d), lambda i, j: (i, 0, 0)),
            pl.BlockSpec((None, 1, d), lambda i, j: (i, 0, 0)),
        ],
        out_specs=pl.BlockSpec((None, ts, d), lambda i, j: (i, j, 0)),
        out_shape=jax.ShapeDtypeStruct((b, s, d), BF16),
        scratch_shapes=[pltpu.VMEM((1, d), F32)],
        compiler_params=_params("arbitrary", "arbitrary"),
        name="norm_mod",
    )(x, g, scale, shift)


def _inproj_kernel(h_ref, wk_ref, cs_ref, o_ref, wb_ref, *maybe_slabs, perm):
    j = pl.program_id(0)
    i = pl.program_id(1)
    kc = wk_ref.shape[0]

    def cast_chunk():
        rows = pl.ds(pl.multiple_of(i * kc, kc), kc)
        wb_ref[j % 2, rows, :] = (wk_ref[...] * cs_ref[...]).astype(BF16)

    @pl.when(j == 0)
    def _():
        cast_chunk()

    @pl.when(j > 0)
    def _():
        cast_chunk()
        vals = jnp.dot(h_ref[...], wb_ref[(j - 1) % 2], preferred_element_type=F32)
        if perm == 1:
            o_ref[...] = vals.astype(BF16)
        else:
            (slab_ref,) = maybe_slabs
            tm, tn = vals.shape
            lanes = slab_ref.shape[2]
            for slab in range(tn // lanes):
                slab_ref[slab] = vals[:, slab * lanes:(slab + 1) * lanes]
            for res in range(perm):
                for slab in range(tn // lanes):
                    rows = slab_ref[slab, pl.ds(res, tm // perm, stride=perm), :]
                    o_ref[res, :, slab * lanes:(slab + 1) * lanes] = rows.astype(BF16)


def _inproj(h, w, colscale, col0, ncols, perm):
    b, s, d = h.shape
    tm = _pick(s, IN_PROJ_TM)
    tn = _pick(ncols, IN_PROJ_TN)
    nm = s // tm
    steps = b * nm
    ncb = ncols // tn
    joff = col0 // tn
    kc = d // steps
    assert col0 % tn == 0 and tm % (8 * perm) == 0 and d % steps == 0 and kc % 16 == 0

    def row(j, i):
        i = jnp.where(j == 0, 0, i)
        return i // nm, i % nm

    def next_col(j):
        return joff + jnp.minimum(j, ncb - 1)

    in_specs = [
        pl.BlockSpec((None, tm, d), lambda j, i: (*row(j, i), 0)),
        pl.BlockSpec((kc, tn), lambda j, i: (i, next_col(j))),
        pl.BlockSpec((1, tn), lambda j, i: (0, next_col(j))),
    ]
    scratch = [pltpu.VMEM((2, d, tn), BF16)]
    if perm == 1:
        out_spec = pl.BlockSpec((None, tm, tn), lambda j, i: (*row(j, i), jnp.maximum(j - 1, 0)))
        out_shape = (b, s, ncols)
    else:
        out_spec = pl.BlockSpec((None, perm, tm // perm, tn),
                                lambda j, i: (row(j, i)[0], 0, row(j, i)[1], jnp.maximum(j - 1, 0)))
        out_shape = (b, perm, s // perm, ncols)
        scratch.append(pltpu.VMEM((tn // HEAD_DIM, tm, HEAD_DIM), F32))
    out = pl.pallas_call(
        functools.partial(_inproj_kernel, perm=perm),
        grid=(ncb + 1, steps),
        in_specs=in_specs,
        out_specs=out_spec,
        out_shape=jax.ShapeDtypeStruct(out_shape, BF16),
        scratch_shapes=scratch,
        compiler_params=_params("arbitrary", "arbitrary"),
        name="in_proj_p%d" % perm,
    )(h, w, colscale)
    return out.reshape(b, s, ncols)


def _head_epilogue(o, g, zg):
    ms = jnp.mean(o * o, axis=-1, keepdims=True)
    return o * lax.rsqrt(ms + EPS) * g * _silu(zg.astype(F32))


def _sb_kernel(q_ref, k_ref, v_ref, z_ref, g_ref, o_ref, acc_ref, carry_ref, *, seq, tile):
    row = lax.broadcasted_iota(jnp.int32, (tile, tile), 0)
    col = lax.broadcasted_iota(jnp.int32, (tile, tile), 1)
    causal = col < row
    upper = jnp.where(row > col, 1.0, 0.0).astype(BF16)

    def key_tile(q, k, v, carry, mask):
        s = _dot_nt(q, k)
        e = jnp.exp2(-jnp.abs(s))
        lb = jnp.minimum(s, 0.0) - jnp.log(1.0 + e) * LOG2E
        lom = lb - s
        if mask is not None:
            lom = jnp.where(mask, lom, 0.0)
        arg = lb + jnp.dot(lom.astype(BF16), upper, preferred_element_type=F32)
        if carry is not None:
            arg = arg + carry
        a = jnp.exp2(arg)
        if mask is not None:
            a = jnp.where(mask, a, 0.0)
        av = jnp.dot(a.astype(BF16), v, preferred_element_type=F32)
        return av, jnp.sum(lom, axis=1, keepdims=True)

    for kj in reversed(range(seq // tile)):
        k0 = kj * tile
        k = k_ref[pl.ds(k0, tile), :]
        v = v_ref[pl.ds(k0, tile), :]
        av, rsum = key_tile(q_ref[pl.ds(k0, tile), :], k, v, None, causal)
        acc_ref[pl.ds(k0, tile), :] = av
        carry_ref[pl.ds(k0, tile), :] = rsum
        if k0 + tile < seq:
            later = pl.ds(k0 + tile, seq - k0 - tile)
            av, rsum = key_tile(q_ref[later, :], k, v, carry_ref[later, :], None)
            acc_ref[later, :] += av
            carry_ref[later, :] += rsum

    for q0 in range(0, seq, tile):
        rows = pl.ds(q0, tile)
        o_ref[rows, :] = _head_epilogue(acc_ref[rows, :], g_ref[...], z_ref[rows, :]).astype(BF16)


def _head_spec(s, n_heads, off):
    return pl.BlockSpec((None, s, HEAD_DIM), lambda i, h: (i, 0, off * n_heads + h))


def _sb_attention(proj, g_sb, n_heads):
    b, s, _ = proj.shape
    tile = _pick(s, 256)
    return pl.pallas_call(
        functools.partial(_sb_kernel, seq=s, tile=tile),
        grid=(b, n_heads),
        in_specs=[_head_spec(s, n_heads, 0), _head_spec(s, n_heads, 1), _head_spec(s, n_heads, 2),
                  _head_spec(s, n_heads, 3), pl.BlockSpec((1, HEAD_DIM), lambda i, h: (0, h))],
        out_specs=pl.BlockSpec((None, s, HEAD_DIM), lambda i, h: (i, 0, h)),
        out_shape=jax.ShapeDtypeStruct((b, s, n_heads * HEAD_DIM), BF16),
        scratch_shapes=[pltpu.VMEM((s, HEAD_DIM), F32), pltpu.VMEM((s, 1), F32)],
        compiler_params=_params("arbitrary", "arbitrary"),
        name="sb_attn",
    )(proj, proj, proj, proj, g_sb)


def _dil_kernel(q_ref, k_ref, v_ref, z_ref, g_ref, slope_ref, onat,
                qf, kf, vf, q16, k16, vo, vo16, num_s, den_s, max_s, *, seq):
    n = DIL_BLOCK
    seg = seq // DIL_PERM
    sub = n // DIL_PERM
    slope = slope_ref[0:1, 0:1]

    def bias_of(dist, window):
        return jnp.where((dist >= 0) & (dist <= window), dist.astype(F32) * (-slope), MASK_BIAS)

    def iota2(nk):
        return (lax.broadcasted_iota(jnp.int32, (n, nk), 0), lax.broadcasted_iota(jnp.int32, (n, nk), 1))

    def attend(q, k, vv, bias):
        s = _dot_nt(q, k) + bias
        m = jnp.max(s, axis=1, keepdims=True)
        p = jnp.exp2(s - m)
        pv = jnp.dot(p.astype(BF16), vv, preferred_element_type=F32)
        return pv[:, :HEAD_DIM], pv[:, HEAD_DIM:], jnp.broadcast_to(m, (n, HEAD_DIM))

    def put(br, idx, res):
        num_s[br, idx, :], den_s[br, idx, :], max_s[br, idx, :] = res

    ones = jnp.ones((seq, HEAD_DIM), BF16)
    vo[:, :HEAD_DIM] = v_ref[...]
    vo[:, HEAD_DIM:] = ones
    vo16[:, HEAD_DIM:] = ones

    a, c = iota2(2 * n)
    tq = (a % sub) * DIL_PERM + a // sub
    bias_prev = bias_of(tq - ((c % (2 * sub)) * DIL_PERM + c // (2 * sub) - n), DIL_PAIRS[0][0])
    a, c = iota2(n)
    bias_first = bias_of(tq[:, :n] - ((c % sub) * DIL_PERM + c // sub), DIL_PAIRS[0][0])

    def gather(ref, start, size):
        return jnp.concatenate([ref[pl.ds(res * seg + start, size), :] for res in range(DIL_PERM)], axis=0)

    for i in range(seq // n):
        q = gather(q_ref, sub * i, sub)
        if i == 0:
            res3 = attend(q, gather(k_ref, 0, sub), gather(vo, 0, sub), bias_first)
        else:
            res3 = attend(q, gather(k_ref, sub * (i - 1), 2 * sub), gather(vo, sub * (i - 1), 2 * sub), bias_prev)
        for res in range(DIL_PERM):
            put(0, pl.ds(res * seg + sub * i, sub), tuple(t[res * sub:(res + 1) * sub] for t in res3))

    r = DIL_PAIRS[1][1]
    assert r == DIL_PERM
    a, c = iota2(2 * n)
    bias_prev = bias_of(r * (a - c + n), DIL_PAIRS[1][0])
    bias_first = bias_prev[:, n:]
    r16 = DIL_PAIRS[2][1]
    step = r16 // DIL_PERM
    assert seq // r16 == n
    a, c = iota2(n)
    bias_16 = bias_of(r16 * (a - c), DIL_PAIRS[2][0])
    chunk = _pick(seg, 256)

    for res in range(DIL_PERM):
        grp = pl.ds(res * seg, seg)
        for i in range(seg // n):
            row0 = res * seg + i * n
            q = q_ref[pl.ds(row0, n), :]
            if i == 0:
                res3 = attend(q, k_ref[pl.ds(row0, n), :], vo[pl.ds(row0, n), :], bias_first)
            else:
                res3 = attend(q, k_ref[pl.ds(row0 - n, 2 * n), :], vo[pl.ds(row0 - n, 2 * n), :], bias_prev)
            put(1, pl.ds(row0, n), res3)

        qf[grp, :] = q_ref[grp, :].astype(F32)
        kf[grp, :] = k_ref[grp, :].astype(F32)
        vf[grp, :] = v_ref[grp, :].astype(F32)
        for sres in range(step):
            src = pl.ds(res * seg + sres, n, stride=step)
            blk = pl.ds((res * step + sres) * n, n)
            q16[blk, :] = qf[src, :].astype(BF16)
            k16[blk, :] = kf[src, :].astype(BF16)
            vo16[blk, :HEAD_DIM] = vf[src, :].astype(BF16)
            put(2, src, attend(q16[blk, :], k16[blk, :], vo16[blk, :], bias_16))

        for l0 in range(0, seg, chunk):
            rows = pl.ds(res * seg + l0, chunk)
            m_all = jnp.maximum(jnp.maximum(max_s[0, rows, :], max_s[1, rows, :]), max_s[2, rows, :])
            num = jnp.zeros((chunk, HEAD_DIM), F32)
            den = jnp.zeros((chunk, HEAD_DIM), F32)
            for br in range(len(DIL_PAIRS)):
                w = jnp.exp2(max_s[br, rows, :] - m_all)
                num = num + num_s[br, rows, :] * w
                den = den + den_s[br, rows, :] * w
            onat[pl.ds(DIL_PERM * l0 + res, chunk, stride=DIL_PERM), :] = _head_epilogue(
                num / den, g_ref[...], z_ref[rows, :])


def _dil_group_kernel(*refs, seq, heads):
    per_head = 6
    o_ref, onat = refs[per_head * heads:per_head * heads + 2]
    scratch = refs[per_head * heads + 2:]
    n_scr = len(scratch) // heads
    for hh in range(heads):
        ins = refs[per_head * hh:per_head * (hh + 1)]
        _dil_kernel(*ins, onat.at[hh], *scratch[n_scr * hh:n_scr * (hh + 1)], seq=seq)
    chunk = _pick(seq, 256)
    for c0 in range(0, seq, chunk):
        rows = pl.ds(c0, chunk)
        o_ref[rows, :] = jnp.concatenate([onat[hh, rows, :] for hh in range(heads)], axis=1).astype(BF16)


def _dil_attention(proj, g_dil, slopes, n_heads):
    b, s, _ = proj.shape
    nbr = len(DIL_PAIRS)
    heads = DIL_HEADS_PER_STEP
    width = heads * HEAD_DIM
    groups = n_heads // heads
    assert n_heads % heads == 0
    rows_f32 = pltpu.VMEM((s, HEAD_DIM), F32)
    rows_bf16 = pltpu.VMEM((s, HEAD_DIM), BF16)
    rows2_bf16 = pltpu.VMEM((s, 2 * HEAD_DIM), BF16)
    per_branch = pltpu.VMEM((nbr, s, HEAD_DIM), F32)

    in_specs, operands = [], []
    for hh in range(heads):
        def head(h, hh=hh):
            return heads * h + hh

        for off in range(4):
            in_specs.append(pl.BlockSpec((None, s, HEAD_DIM),
                                         lambda i, h, off=off, head=head: (i, 0, off * n_heads + head(h))))
        in_specs.append(pl.BlockSpec((1, HEAD_DIM), lambda i, h, head=head: (0, head(h))))
        in_specs.append(pl.BlockSpec((None, 1, HEAD_DIM), lambda i, h, head=head: (head(h), 0, 0)))
        operands += [proj, proj, proj, proj, g_dil, slopes]

    return pl.pallas_call(
        functools.partial(_dil_group_kernel, seq=s, heads=heads),
        grid=(b, groups),
        in_specs=in_specs,
        out_specs=pl.BlockSpec((None, s, width), lambda i, h: (i, 0, h)),
        out_shape=jax.ShapeDtypeStruct((b, s, n_heads * HEAD_DIM), BF16),
        scratch_shapes=[pltpu.VMEM((heads, s, HEAD_DIM), F32)]
        + heads * [rows_f32, rows_f32, rows_f32, rows_bf16, rows_bf16, rows2_bf16, rows2_bf16,
                   per_branch, per_branch, per_branch],
        compiler_params=_params("arbitrary", "arbitrary"),
        name="dil_attn",
    )(*operands)


def _outproj_kernel(ya_ref, yb_ref, wk_ref, x_ref, gate_ref, o_ref, wb_ref):
    j = pl.program_id(0)
    i = pl.program_id(1)
    kc = wk_ref.shape[0]
    ka = ya_ref.shape[1]

    def cast_chunk():
        wb_ref[j % 2, pl.ds(pl.multiple_of(i * kc, kc), kc), :] = wk_ref[...].astype(BF16)

    @pl.when(j == 0)
    def _():
        cast_chunk()

    @pl.when(j > 0)
    def _():
        cast_chunk()
        cur = (j - 1) % 2
        acc = jnp.dot(ya_ref[...], wb_ref[cur, :ka, :], preferred_element_type=F32)
        acc = acc + jnp.dot(yb_ref[...], wb_ref[cur, ka:, :], preferred_element_type=F32)
        o_ref[...] = x_ref[...] + gate_ref[...] * acc


def _outproj(y_sb, y_dl, w_out, x, gate):
    b, s, d = x.shape
    ka = y_sb.shape[2]
    kb = y_dl.shape[2]
    tm = _pick(s, OUT_PROJ_TM)
    tn = _pick(d, OUT_PROJ_TN)
    nm = s // tm
    steps = b * nm
    ncb = d // tn
    kc = (ka + kb) // steps
    assert (ka + kb) % steps == 0 and kc % 16 == 0

    def row(j, i):
        i = jnp.where(j == 0, 0, i)
        return i // nm, i % nm

    def col(j):
        return jnp.maximum(j - 1, 0)

    return pl.pallas_call(
        _outproj_kernel,
        grid=(ncb + 1, steps),
        in_specs=[
            pl.BlockSpec((None, tm, ka), lambda j, i: (*row(j, i), 0)),
            pl.BlockSpec((None, tm, kb), lambda j, i: (*row(j, i), 0)),
            pl.BlockSpec((kc, tn), lambda j, i: (i, jnp.minimum(j, ncb - 1))),
            pl.BlockSpec((None, tm, tn), lambda j, i: (*row(j, i), col(j))),
            pl.BlockSpec((None, 1, tn), lambda j, i: (row(j, i)[0], 0, col(j))),
        ],
        out_specs=pl.BlockSpec((None, tm, tn), lambda j, i: (*row(j, i), col(j))),
        out_shape=jax.ShapeDtypeStruct((b, s, d), F32),
        scratch_shapes=[pltpu.VMEM((2, ka + kb, tn), BF16)],
        compiler_params=pltpu.CompilerParams(dimension_semantics=("arbitrary", "arbitrary"),
                                             vmem_limit_bytes=OUT_PROJ_VMEM_LIMIT),
        name="out_proj",
    )(y_sb, y_dl, w_out, x, gate)


def _final_norm_kernel(x_ref, g_ref, o_ref):
    x = x_ref[...]
    ms = jnp.mean(x * x, axis=-1, keepdims=True)
    o_ref[...] = x * lax.rsqrt(ms + EPS) * g_ref[...]


def _final_norm(x, g):
    b, s, d = x.shape
    ts = _pick(s, NORM_ROWS)
    return pl.pallas_call(
        _final_norm_kernel,
        grid=(b, s // ts),
        in_specs=[pl.BlockSpec((None, ts, d), lambda i, j: (i, j, 0)),
                  pl.BlockSpec((1, d), lambda i, j: (0, 0))],
        out_specs=pl.BlockSpec((None, ts, d), lambda i, j: (i, j, 0)),
        out_shape=jax.ShapeDtypeStruct((b, s, d), F32),
        compiler_params=_params("arbitrary", "arbitrary"),
        name="final_norm",
    )(x, g)


def kernel(x, c, w_ada, b_ada, g_norm, w_in, g_sb, g_dil, w_out, g_final):
    b, s, d = x.shape
    depth = w_ada.shape[0]
    n_sb = g_sb.shape[1] // HEAD_DIM
    n_dil = g_dil.shape[1] // HEAD_DIM
    sb_w = n_sb * HEAD_DIM
    dil_w = n_dil * HEAD_DIM
    assert s % DIL_PAIRS[-1][0] == 0 and sb_w == dil_w

    qscale = LOG2E / math.sqrt(HEAD_DIM)
    colscale = jnp.concatenate([
        jnp.full((sb_w,), qscale, F32), jnp.ones((3 * sb_w,), F32),
        jnp.full((dil_w,), qscale, F32), jnp.ones((3 * dil_w,), F32)])[None, :]
    slopes = jnp.exp2(-ALIBI_MAX_BIAS * jnp.arange(1, n_dil + 1, dtype=F32) / n_dil) * LOG2E
    slopes = jnp.broadcast_to(slopes[:, None, None], (n_dil, 1, HEAD_DIM))

    pad = 16
    c_pad = jnp.zeros((pad, d), F32).at[:b].set(c.astype(F32))
    for layer in range(depth):
        mod = _ada(c_pad, w_ada[layer], b_ada[layer][None, :])[:b]
        shift, scale, gate = (mod[:, i * d:(i + 1) * d][:, None, :] for i in range(3))
        h = _norm_mod(x, g_norm[layer][None, :], scale, shift)
        proj_sb = _inproj(h, w_in[layer], colscale, 0, 4 * sb_w, 1)
        proj_dl = _inproj(h, w_in[layer], colscale, 4 * sb_w, 4 * dil_w, DIL_PERM)
        y_sb = _sb_attention(proj_sb, g_sb[layer][None, :], n_sb)
        y_dl = _dil_attention(proj_dl, g_dil[layer][None, :], slopes, n_dil)
        x = _outproj(y_sb, y_dl, w_out[layer], x, gate)
    return _final_norm(x, g_final[None, :])
```

```python
import functools
import math

import jax
import jax.numpy as jnp
from jax import lax
from jax.experimental import pallas as pl
from jax.experimental.pallas import tpu as pltpu

HEAD_DIM = 128
EPS = 1e-6
LOG2E = 1.4426950408889634
ALIBI_MAX_BIAS = 8.0
DIL_PAIRS = ((128, 1), (512, 4), (2048, 16))
DIL_BLOCK = 128
DIL_PERM = 4
MASK_BIAS = -1e30
V7X_VMEM_BYTES = 64 * 1024 * 1024
VMEM_LIMIT = V7X_VMEM_BYTES - 8 * 1024 * 1024
IN_PROJ_TM, IN_PROJ_TN = 1024, 1024
OUT_PROJ_TM, OUT_PROJ_TN = 1024, 1024
OUT_PROJ_VMEM_LIMIT = V7X_VMEM_BYTES - 2 * 1024 * 1024
NORM_ROWS = 512
DIL_HEADS_PER_STEP = 2

F32 = jnp.float32
BF16 = jnp.bfloat16


def _params(*sem):
    return pltpu.CompilerParams(dimension_semantics=sem, vmem_limit_bytes=VMEM_LIMIT)


def _pick(n, pref):
    t = min(pref, n)
    while n % t:
        t //= 2
    return t


def _silu(x):
    return x / (1.0 + jnp.exp(-x))


def _dot_nt(a, b):
    return lax.dot_general(a, b, (((1,), (1,)), ((), ())), preferred_element_type=F32)


def _ada_kernel(c_ref, w_ref, b_ref, o_ref):
    cs = _silu(c_ref[...])
    acc = jnp.dot(cs.astype(BF16), w_ref[...].astype(BF16), preferred_element_type=F32)
    o_ref[...] = acc + b_ref[...]


def _ada(c_pad, w, b):
    m, d = c_pad.shape
    n = w.shape[1]
    tn = _pick(n, 512)
    return pl.pallas_call(
        _ada_kernel,
        grid=(n // tn,),
        in_specs=[
            pl.BlockSpec((m, d), lambda j: (0, 0)),
            pl.BlockSpec((d, tn), lambda j: (0, j)),
            pl.BlockSpec((1, tn), lambda j: (0, j)),
        ],
        out_specs=pl.BlockSpec((m, tn), lambda j: (0, j)),
        out_shape=jax.ShapeDtypeStruct((m, n), F32),
        compiler_params=_params("arbitrary"),
        name="ada_ln",
    )(c_pad, w, b)


def _norm_mod_kernel(x_ref, g_ref, scale_ref, shift_ref, h_ref, gs_ref):
    @pl.when(pl.program_id(1) == 0)
    def _():
        gs_ref[...] = g_ref[...] * (1.0 + scale_ref[...])

    x = x_ref[...]
    ms = jnp.mean(x * x, axis=-1, keepdims=True)
    h_ref[...] = (x * lax.rsqrt(ms + EPS) * gs_ref[...] + shift_ref[...]).astype(BF16)


def _norm_mod(x, g, scale, shift):
    b, s, d = x.shape
    ts = _pick(s, NORM_ROWS)
    return pl.pallas_call(
        _norm_mod_kernel,
        grid=(b, s // ts),
        in_specs=[
            pl.BlockSpec((None, ts, d), lambda i, j: (i, j, 0)),
            pl.BlockSpec((1, d), lambda i, j: (0, 0)),
            pl.BlockSpec((None, 1, d), lambda i, j: (i, 0, 0)),
            pl.BlockSpec((None, 1, d), lambda i, j: (i, 0, 0)),
        ],
        out_specs=pl.BlockSpec((None, ts, d), lambda i, j: (i, j, 0)),
        out_shape=jax.ShapeDtypeStruct((b, s, d), BF16),
        scratch_shapes=[pltpu.VMEM((1, d), F32)],
        compiler_params=_params("arbitrary", "arbitrary"),
        name="norm_mod",
    )(x, g, scale, shift)


def _inproj_kernel(h_ref, wk_ref, cs_ref, o_ref, wb_ref, *maybe_slabs, perm):
    j = pl.program_id(0)
    i = pl.program_id(1)
    kc = wk_ref.shape[0]

    def cast_chunk():
        rows = pl.ds(pl.multiple_of(i * kc, kc), kc)
        wb_ref[j % 2, rows, :] = (wk_ref[...] * cs_ref[...]).astype(BF16)

    @pl.when(j == 0)
    def _():
        cast_chunk()

    @pl.when(j > 0)
    def _():
        cast_chunk()
        vals = jnp.dot(h_ref[...], wb_ref[(j - 1) % 2], preferred_element_type=F32)
        if perm == 1:
            o_ref[...] = vals.astype(BF16)
        else:
            (slab_ref,) = maybe_slabs
            tm, tn = vals.shape
            lanes = slab_ref.shape[2]
            for slab in range(tn // lanes):
                slab_ref[slab] = vals[:, slab * lanes:(slab + 1) * lanes]
            for res in range(perm):
                for slab in range(tn // lanes):
                    rows = slab_ref[slab, pl.ds(res, tm // perm, stride=perm), :]
                    o_ref[res, :, slab * lanes:(slab + 1) * lanes] = rows.astype(BF16)


def _inproj(h, w, colscale, col0, ncols, perm):
    b, s, d = h.shape
    tm = _pick(s, IN_PROJ_TM)
    tn = _pick(ncols, IN_PROJ_TN)
    nm = s // tm
    steps = b * nm
    ncb = ncols // tn
    joff = col0 // tn
    kc = d // steps
    assert col0 % tn == 0 and tm % (8 * perm) == 0 and d % steps == 0 and kc % 16 == 0

    def row(j, i):
        i = jnp.where(j == 0, 0, i)
        return i // nm, i % nm

    def next_col(j):
        return joff + jnp.minimum(j, ncb - 1)

    in_specs = [
        pl.BlockSpec((None, tm, d), lambda j, i: (*row(j, i), 0)),
        pl.BlockSpec((kc, tn), lambda j, i: (i, next_col(j))),
        pl.BlockSpec((1, tn), lambda j, i: (0, next_col(j))),
    ]
    scratch = [pltpu.VMEM((2, d, tn), BF16)]
    if perm == 1:
        out_spec = pl.BlockSpec((None, tm, tn), lambda j, i: (*row(j, i), jnp.maximum(j - 1, 0)))
        out_shape = (b, s, ncols)
    else:
        out_spec = pl.BlockSpec((None, perm, tm // perm, tn),
                                lambda j, i: (row(j, i)[0], 0, row(j, i)[1], jnp.maximum(j - 1, 0)))
        out_shape = (b, perm, s // perm, ncols)
        scratch.append(pltpu.VMEM((tn // HEAD_DIM, tm, HEAD_DIM), F32))
    out = pl.pallas_call(
        functools.partial(_inproj_kernel, perm=perm),
        grid=(ncb + 1, steps),
        in_specs=in_specs,
        out_specs=out_spec,
        out_shape=jax.ShapeDtypeStruct(out_shape, BF16),
        scratch_shapes=scratch,
        compiler_params=_params("arbitrary", "arbitrary"),
        name="in_proj_p%d" % perm,
    )(h, w, colscale)
    return out.reshape(b, s, ncols)


def _head_epilogue(o, g, zg):
    ms = jnp.mean(o * o, axis=-1, keepdims=True)
    return o * lax.rsqrt(ms + EPS) * g * _silu(zg.astype(F32))


def _sb_kernel(q_ref, k_ref, v_ref, z_ref, g_ref, o_ref, acc_ref, carry_ref, *, seq, tile):
    row = lax.broadcasted_iota(jnp.int32, (tile, tile), 0)
    col = lax.broadcasted_iota(jnp.int32, (tile, tile), 1)
    causal = col < row
    upper = jnp.where(row >= col, 1.0, 0.0).astype(BF16)

    def key_tile(q, k, v, carry, mask):
        s = _dot_nt(q, k)
        ns = -s
        e = jnp.exp2(jnp.minimum(s, ns))
        lom = jnp.minimum(ns, 0.0) - jnp.log(1.0 + e) * LOG2E
        if mask is not None:
            lom = jnp.where(mask, lom, 0.0)
        arg = s + jnp.dot(lom.astype(BF16), upper, preferred_element_type=F32)
        if carry is not None:
            arg = arg + carry
        a = jnp.exp2(arg)
        if mask is not None:
            a = jnp.where(mask, a, 0.0)
        av = jnp.dot(a.astype(BF16), v, preferred_element_type=F32)
        return av, jnp.sum(lom, axis=1, keepdims=True)

    for kj in reversed(range(seq // tile)):
        k0 = kj * tile
        k = k_ref[pl.ds(k0, tile), :]
        v = v_ref[pl.ds(k0, tile), :]
        av, rsum = key_tile(q_ref[pl.ds(k0, tile), :], k, v, None, causal)
        acc_ref[pl.ds(k0, tile), :] = av
        carry_ref[pl.ds(k0, tile), :] = rsum
        if k0 + tile < seq:
            later = pl.ds(k0 + tile, seq - k0 - tile)
            av, rsum = key_tile(q_ref[later, :], k, v, carry_ref[later, :], None)
            acc_ref[later, :] += av
            carry_ref[later, :] += rsum

    for q0 in range(0, seq, tile):
        rows = pl.ds(q0, tile)
        o_ref[rows, :] = _head_epilogue(acc_ref[rows, :], g_ref[...], z_ref[rows, :]).astype(BF16)


def _head_spec(s, n_heads, off):
    return pl.BlockSpec((None, s, HEAD_DIM), lambda i, h: (i, 0, off * n_heads + h))


def _sb_attention(proj, g_sb, n_heads):
    b, s, _ = proj.shape
    tile = _pick(s, 256)
    return pl.pallas_call(
        functools.partial(_sb_kernel, seq=s, tile=tile),
        grid=(b, n_heads),
        in_specs=[_head_spec(s, n_heads, 0), _head_spec(s, n_heads, 1), _head_spec(s, n_heads, 2),
                  _head_spec(s, n_heads, 3), pl.BlockSpec((1, HEAD_DIM), lambda i, h: (0, h))],
        out_specs=pl.BlockSpec((None, s, HEAD_DIM), lambda i, h: (i, 0, h)),
        out_shape=jax.ShapeDtypeStruct((b, s, n_heads * HEAD_DIM), BF16),
        scratch_shapes=[pltpu.VMEM((s, HEAD_DIM), F32), pltpu.VMEM((s, 1), F32)],
        compiler_params=_params("arbitrary", "arbitrary"),
        name="sb_attn",
    )(proj, proj, proj, proj, g_sb)


def _dil_kernel(q_ref, k_ref, v_ref, z_ref, g_ref, slope_ref, onat,
                qf, kf, vf, q16, k16, vo, vo16, num_s, den_s, max_s, *, seq):
    n = DIL_BLOCK
    seg = seq // DIL_PERM
    sub = n // DIL_PERM
    slope = slope_ref[0:1, 0:1]

    def bias_of(dist, window):
        return jnp.where((dist >= 0) & (dist <= window), dist.astype(F32) * (-slope), MASK_BIAS)

    def iota2(nk):
        return (lax.broadcasted_iota(jnp.int32, (n, nk), 0), lax.broadcasted_iota(jnp.int32, (n, nk), 1))

    def attend(q, k, vv, bias):
        s = _dot_nt(q, k) + bias
        m = jnp.max(s, axis=1, keepdims=True)
        p = jnp.exp2(s - m)
        pv = jnp.dot(p.astype(BF16), vv, preferred_element_type=F32)
        return pv[:, :HEAD_DIM], pv[:, HEAD_DIM:], jnp.broadcast_to(m, (n, HEAD_DIM))

    def put(br, idx, res):
        num_s[br, idx, :], den_s[br, idx, :], max_s[br, idx, :] = res

    ones = jnp.ones((seq, HEAD_DIM), BF16)
    vo[:, :HEAD_DIM] = v_ref[...]
    vo[:, HEAD_DIM:] = ones
    vo16[:, HEAD_DIM:] = ones

    a, c = iota2(2 * n)
    tq = (a % sub) * DIL_PERM + a // sub
    bias_prev = bias_of(tq - ((c % (2 * sub)) * DIL_PERM + c // (2 * sub) - n), DIL_PAIRS[0][0])
    a, c = iota2(n)
    bias_first = bias_of(tq[:, :n] - ((c % sub) * DIL_PERM + c // sub), DIL_PAIRS[0][0])

    def gather(ref, start, size):
        return jnp.concatenate([ref[pl.ds(res * seg + start, size), :] for res in range(DIL_PERM)], axis=0)

    for i in range(seq // n):
        q = gather(q_ref, sub * i, sub)
        if i == 0:
            res3 = attend(q, gather(k_ref, 0, sub), gather(vo, 0, sub), bias_first)
        else:
            res3 = attend(q, gather(k_ref, sub * (i - 1), 2 * sub), gather(vo, sub * (i - 1), 2 * sub), bias_prev)
        for res in range(DIL_PERM):
            put(0, pl.ds(res * seg + sub * i, sub), tuple(t[res * sub:(res + 1) * sub] for t in res3))

    r = DIL_PAIRS[1][1]
    assert r == DIL_PERM
    a, c = iota2(2 * n)
    bias_prev = bias_of(r * (a - c + n), DIL_PAIRS[1][0])
    bias_first = bias_prev[:, n:]
    r16 = DIL_PAIRS[2][1]
    step = r16 // DIL_PERM
    assert seq // r16 == n
    a, c = iota2(n)
    bias_16 = bias_of(r16 * (a - c), DIL_PAIRS[2][0])
    chunk = _pick(seg, 256)

    for res in range(DIL_PERM):
        grp = pl.ds(res * seg, seg)
        for i in range(seg // n):
            row0 = res * seg + i * n
            q = q_ref[pl.ds(row0, n), :]
            if i == 0:
                res3 = attend(q, k_ref[pl.ds(row0, n), :], vo[pl.ds(row0, n), :], bias_first)
            else:
                res3 = attend(q, k_ref[pl.ds(row0 - n, 2 * n), :], vo[pl.ds(row0 - n, 2 * n), :], bias_prev)
            put(1, pl.ds(row0, n), res3)

        qf[grp, :] = q_ref[grp, :].astype(F32)
        kf[grp, :] = k_ref[grp, :].astype(F32)
        vf[grp, :] = v_ref[grp, :].astype(F32)
        for sres in range(step):
            src = pl.ds(res * seg + sres, n, stride=step)
            blk = pl.ds((res * step + sres) * n, n)
            q16[blk, :] = qf[src, :].astype(BF16)
            k16[blk, :] = kf[src, :].astype(BF16)
            vo16[blk, :HEAD_DIM] = vf[src, :].astype(BF16)
            put(2, src, attend(q16[blk, :], k16[blk, :], vo16[blk, :], bias_16))

        for l0 in range(0, seg, chunk):
            rows = pl.ds(res * seg + l0, chunk)
            m_all = jnp.maximum(jnp.maximum(max_s[0, rows, :], max_s[1, rows, :]), max_s[2, rows, :])
            num = jnp.zeros((chunk, HEAD_DIM), F32)
            den = jnp.zeros((chunk, HEAD_DIM), F32)
            for br in range(len(DIL_PAIRS)):
                w = jnp.exp2(max_s[br, rows, :] - m_all)
                num = num + num_s[br, rows, :] * w
                den = den + den_s[br, rows, :] * w
            onat[pl.ds(DIL_PERM * l0 + res, chunk, stride=DIL_PERM), :] = _head_epilogue(
                num / den, g_ref[...], z_ref[rows, :])


def _dil_group_kernel(*refs, seq, heads):
    per_head = 6
    o_ref, onat = refs[per_head * heads:per_head * heads + 2]
    scratch = refs[per_head * heads + 2:]
    n_scr = len(scratch) // heads
    for hh in range(heads):
        ins = refs[per_head * hh:per_head * (hh + 1)]
        _dil_kernel(*ins, onat.at[hh], *scratch[n_scr * hh:n_scr * (hh + 1)], seq=seq)
    chunk = _pick(seq, 256)
    for c0 in range(0, seq, chunk):
        rows = pl.ds(c0, chunk)
        o_ref[rows, :] = jnp.concatenate([onat[hh, rows, :] for hh in range(heads)], axis=1).astype(BF16)


def _dil_attention(proj, g_dil, slopes, n_heads):
    b, s, _ = proj.shape
    nbr = len(DIL_PAIRS)
    heads = DIL_HEADS_PER_STEP
    width = heads * HEAD_DIM
    groups = n_heads // heads
    assert n_heads % heads == 0
    rows_f32 = pltpu.VMEM((s, HEAD_DIM), F32)
    rows_bf16 = pltpu.VMEM((s, HEAD_DIM), BF16)
    rows2_bf16 = pltpu.VMEM((s, 2 * HEAD_DIM), BF16)
    per_branch = pltpu.VMEM((nbr, s, HEAD_DIM), F32)

    in_specs, operands = [], []
    for hh in range(heads):
        def head(h, hh=hh):
            return heads * h + hh

        for off in range(4):
            in_specs.append(pl.BlockSpec((None, s, HEAD_DIM),
                                         lambda i, h, off=off, head=head: (i, 0, off * n_heads + head(h))))
        in_specs.append(pl.BlockSpec((1, HEAD_DIM), lambda i, h, head=head: (0, head(h))))
        in_specs.append(pl.BlockSpec((None, 1, HEAD_DIM), lambda i, h, head=head: (head(h), 0, 0)))
        operands += [proj, proj, proj, proj, g_dil, slopes]

    return pl.pallas_call(
        functools.partial(_dil_group_kernel, seq=s, heads=heads),
        grid=(b, groups),
        in_specs=in_specs,
        out_specs=pl.BlockSpec((None, s, width), lambda i, h: (i, 0, h)),
        out_shape=jax.ShapeDtypeStruct((b, s, n_heads * HEAD_DIM), BF16),
        scratch_shapes=[pltpu.VMEM((heads, s, HEAD_DIM), F32)]
        + heads * [rows_f32, rows_f32, rows_f32, rows_bf16, rows_bf16, rows2_bf16, rows2_bf16,
                   per_branch, per_branch, per_branch],
        compiler_params=_params("arbitrary", "arbitrary"),
        name="dil_attn",
    )(*operands)


def _outproj_kernel(ya_ref, yb_ref, wk_ref, x_ref, gate_ref, o_ref, wb_ref):
    j = pl.program_id(0)
    i = pl.program_id(1)
    kc = wk_ref.shape[0]
    ka = ya_ref.shape[1]

    def cast_chunk():
        wb_ref[j % 2, pl.ds(pl.multiple_of(i * kc, kc), kc), :] = wk_ref[...].astype(BF16)

    @pl.when(j == 0)
    def _():
        cast_chunk()

    @pl.when(j > 0)
    def _():
        cast_chunk()
        cur = (j - 1) % 2
        acc = jnp.dot(ya_ref[...], wb_ref[cur, :ka, :], preferred_element_type=F32)
        acc = acc + jnp.dot(yb_ref[...], wb_ref[cur, ka:, :], preferred_element_type=F32)
        o_ref[...] = x_ref[...] + gate_ref[...] * acc


def _outproj(y_sb, y_dl, w_out, x, gate):
    b, s, d = x.shape
    ka = y_sb.shape[2]
    kb = y_dl.shape[2]
    tm = _pick(s, OUT_PROJ_TM)
    tn = _pick(d, OUT_PROJ_TN)
    nm = s // tm
    steps = b * nm
    ncb = d // tn
    kc = (ka + kb) // steps
    assert (ka + kb) % steps == 0 and kc % 16 == 0

    def row(j, i):
        i = jnp.where(j == 0, 0, i)
        return i // nm, i % nm

    def col(j):
        return jnp.maximum(j - 1, 0)

    return pl.pallas_call(
        _outproj_kernel,
        grid=(ncb + 1, steps),
        in_specs=[
            pl.BlockSpec((None, tm, ka), lambda j, i: (*row(j, i), 0)),
            pl.BlockSpec((None, tm, kb), lambda j, i: (*row(j, i), 0)),
            pl.BlockSpec((kc, tn), lambda j, i: (i, jnp.minimum(j, ncb - 1))),
            pl.BlockSpec((None, tm, tn), lambda j, i: (*row(j, i), col(j))),
            pl.BlockSpec((None, 1, tn), lambda j, i: (row(j, i)[0], 0, col(j))),
        ],
        out_specs=pl.BlockSpec((None, tm, tn), lambda j, i: (*row(j, i), col(j))),
        out_shape=jax.ShapeDtypeStruct((b, s, d), F32),
        scratch_shapes=[pltpu.VMEM((2, ka + kb, tn), BF16)],
        compiler_params=pltpu.CompilerParams(dimension_semantics=("arbitrary", "arbitrary"),
                                             vmem_limit_bytes=OUT_PROJ_VMEM_LIMIT),
        name="out_proj",
    )(y_sb, y_dl, w_out, x, gate)


def _final_norm_kernel(x_ref, g_ref, o_ref):
    x = x_ref[...]
    ms = jnp.mean(x * x, axis=-1, keepdims=True)
    o_ref[...] = x * lax.rsqrt(ms + EPS) * g_ref[...]


def _final_norm(x, g):
    b, s, d = x.shape
    ts = _pick(s, NORM_ROWS)
    return pl.pallas_call(
        _final_norm_kernel,
        grid=(b, s // ts),
        in_specs=[pl.BlockSpec((None, ts, d), lambda i, j: (i, j, 0)),
                  pl.BlockSpec((1, d), lambda i, j: (0, 0))],
        out_specs=pl.BlockSpec((None, ts, d), lambda i, j: (i, j, 0)),
        out_shape=jax.ShapeDtypeStruct((b, s, d), F32),
        compiler_params=_params("arbitrary", "arbitrary"),
        name="final_norm",
    )(x, g)


def kernel(x, c, w_ada, b_ada, g_norm, w_in, g_sb, g_dil, w_out, g_final):
    b, s, d = x.shape
    depth = w_ada.shape[0]
    n_sb = g_sb.shape[1] // HEAD_DIM
    n_dil = g_dil.shape[1] // HEAD_DIM
    sb_w = n_sb * HEAD_DIM
    dil_w = n_dil * HEAD_DIM
    assert s % DIL_PAIRS[-1][0] == 0 and sb_w == dil_w

    qscale = LOG2E / math.sqrt(HEAD_DIM)
    colscale = jnp.concatenate([
        jnp.full((sb_w,), qscale, F32), jnp.ones((3 * sb_w,), F32),
        jnp.full((dil_w,), qscale, F32), jnp.ones((3 * dil_w,), F32)])[None, :]
    slopes = jnp.exp2(-ALIBI_MAX_BIAS * jnp.arange(1, n_dil + 1, dtype=F32) / n_dil) * LOG2E
    slopes = jnp.broadcast_to(slopes[:, None, None], (n_dil, 1, HEAD_DIM))

    pad = 16
    c_pad = jnp.zeros((pad, d), F32).at[:b].set(c.astype(F32))
    for layer in range(depth):
        mod = _ada(c_pad, w_ada[layer], b_ada[layer][None, :])[:b]
        shift, scale, gate = (mod[:, i * d:(i + 1) * d][:, None, :] for i in range(3))
        h = _norm_mod(x, g_norm[layer][None, :], scale, shift)
        proj_sb = _inproj(h, w_in[layer], colscale, 0, 4 * sb_w, 1)
        proj_dl = _inproj(h, w_in[layer], colscale, 4 * sb_w, 4 * dil_w, DIL_PERM)
        y_sb = _sb_attention(proj_sb, g_sb[layer][None, :], n_sb)
        y_dl = _dil_attention(proj_dl, g_dil[layer][None, :], slopes, n_dil)
        x = _outproj(y_sb, y_dl, w_out[layer], x, gate)
    return _final_norm(x, g_final[None, :])
```

```python
import functools
import math

import jax
import jax.numpy as jnp
from jax import lax
from jax.experimental import pallas as pl
from jax.experimental.pallas import tpu as pltpu

HEAD_DIM = 128
EPS = 1e-6
LOG2E = 1.4426950408889634
ALIBI_MAX_BIAS = 8.0
DIL_PAIRS = ((128, 1), (512, 4), (2048, 16))
DIL_BLOCK = 128
DIL_PERM = 4
MASK_BIAS = -1e30
V7X_VMEM_BYTES = 64 * 1024 * 1024
V7X_MXU_DIM = 256
BF16_ROWS_PER_VREG = 16
VMEM_LIMIT = V7X_VMEM_BYTES - 8 * 1024 * 1024
IN_PROJ_TM, IN_PROJ_TN = 1024, 1024
OUT_PROJ_TM, OUT_PROJ_TN = 1024, 1024
OUT_PROJ_VMEM_LIMIT = V7X_VMEM_BYTES - 2 * 1024 * 1024
ADA_TN = 512
NORM_ROWS = 512
ROW_CHUNK = V7X_MXU_DIM
DIL_HEADS_PER_STEP = 2

F32 = jnp.float32
BF16 = jnp.bfloat16


def _params(*sem):
    return pltpu.CompilerParams(dimension_semantics=sem, vmem_limit_bytes=VMEM_LIMIT)


def _pick(n, pref):
    t = min(pref, n)
    while n % t:
        t //= 2
    return t


def _silu(x):
    return x / (1.0 + jnp.exp(-x))


def _dot_nt(a, b):
    return lax.dot_general(a, b, (((1,), (1,)), ((), ())), preferred_element_type=F32)


def _ada_kernel(c_ref, w_ref, b_ref, o_ref):
    cs = _silu(c_ref[...])
    acc = jnp.dot(cs.astype(BF16), w_ref[...].astype(BF16), preferred_element_type=F32)
    o_ref[...] = acc + b_ref[...]


def _ada(c_pad, w, b):
    m, d = c_pad.shape
    n = w.shape[1]
    tn = _pick(n, ADA_TN)
    return pl.pallas_call(
        _ada_kernel,
        grid=(n // tn,),
        in_specs=[
            pl.BlockSpec((m, d), lambda j: (0, 0)),
            pl.BlockSpec((d, tn), lambda j: (0, j)),
            pl.BlockSpec((1, tn), lambda j: (0, j)),
        ],
        out_specs=pl.BlockSpec((m, tn), lambda j: (0, j)),
        out_shape=jax.ShapeDtypeStruct((m, n), F32),
        compiler_params=_params("arbitrary"),
        name="ada_ln",
    )(c_pad, w, b)


def _norm_mod_kernel(x_ref, g_ref, scale_ref, shift_ref, h_ref, gs_ref):
    @pl.when(pl.program_id(1) == 0)
    def _():
        gs_ref[...] = g_ref[...] * (1.0 + scale_ref[...])

    x = x_ref[...]
    ms = jnp.mean(x * x, axis=-1, keepdims=True)
    h_ref[...] = (x * lax.rsqrt(ms + EPS) * gs_ref[...] + shift_ref[...]).astype(BF16)


def _norm_mod(x, g, scale, shift):
    b, s, d = x.shape
    ts = _pick(s, NORM_ROWS)
    return pl.pallas_call(
        _norm_mod_kernel,
        grid=(b, s // ts),
        in_specs=[
            pl.BlockSpec((None, ts, d), lambda i, j: (i, j, 0)),
            pl.BlockSpec((1, d), lambda i, j: (0, 0)),
            pl.BlockSpec((None, 1, d), lambda i, j: (i, 0, 0)),
            pl.BlockSpec((None, 1, d), lambda i, j: (i, 0, 0)),
        ],
        out_specs=pl.BlockSpec((None, ts, d), lambda i, j: (i, j, 0)),
        out_shape=jax.ShapeDtypeStruct((b, s, d), BF16),
        scratch_shapes=[pltpu.VMEM((1, d), F32)],
        compiler_params=_params("arbitrary", "arbitrary"),
        name="norm_mod",
    )(x, g, scale, shift)


def _inproj_kernel(h_ref, wk_ref, cs_ref, o_ref, wb_ref, *maybe_slabs, perm):
    j = pl.program_id(0)
    i = pl.program_id(1)
    kc = wk_ref.shape[0]

    def cast_chunk():
        rows = pl.ds(pl.multiple_of(i * kc, kc), kc)
        wb_ref[j % 2, rows, :] = (wk_ref[...] * cs_ref[...]).astype(BF16)

    @pl.when(j == 0)
    def _():
        cast_chunk()

    @pl.when(j > 0)
    def _():
        cast_chunk()
        vals = jnp.dot(h_ref[...], wb_ref[(j - 1) % 2], preferred_element_type=F32)
        if perm == 1:
            o_ref[...] = vals.astype(BF16)
        else:
            (slab_ref,) = maybe_slabs
            tm, tn = vals.shape
            lanes = slab_ref.shape[2]
            for slab in range(tn // lanes):
                slab_ref[slab] = vals[:, slab * lanes:(slab + 1) * lanes]
            for res in range(perm):
                for slab in range(tn // lanes):
                    rows = slab_ref[slab, pl.ds(res, tm // perm, stride=perm), :]
                    o_ref[res, :, slab * lanes:(slab + 1) * lanes] = rows.astype(BF16)


def _inproj(h, w, colscale, col0, ncols, perm):
    b, s, d = h.shape
    tm = _pick(s, IN_PROJ_TM)
    tn = _pick(ncols, IN_PROJ_TN)
    nm = s // tm
    steps = b * nm
    ncb = ncols // tn
    joff = col0 // tn
    kc = d // steps
    assert col0 % tn == 0 and tm % (BF16_ROWS_PER_VREG * perm) == 0
    assert d % steps == 0 and kc % BF16_ROWS_PER_VREG == 0

    def row(j, i):
        i = jnp.where(j == 0, 0, i)
        return i // nm, i % nm

    def next_col(j):
        return joff + jnp.minimum(j, ncb - 1)

    in_specs = [
        pl.BlockSpec((None, tm, d), lambda j, i: (*row(j, i), 0)),
        pl.BlockSpec((kc, tn), lambda j, i: (i, next_col(j))),
        pl.BlockSpec((1, tn), lambda j, i: (0, next_col(j))),
    ]
    scratch = [pltpu.VMEM((2, d, tn), BF16)]
    if perm == 1:
        out_spec = pl.BlockSpec((None, tm, tn), lambda j, i: (*row(j, i), jnp.maximum(j - 1, 0)))
        out_shape = (b, s, ncols)
    else:
        out_spec = pl.BlockSpec((None, perm, tm // perm, tn),
                                lambda j, i: (row(j, i)[0], 0, row(j, i)[1], jnp.maximum(j - 1, 0)))
        out_shape = (b, perm, s // perm, ncols)
        scratch.append(pltpu.VMEM((tn // HEAD_DIM, tm, HEAD_DIM), F32))
    out = pl.pallas_call(
        functools.partial(_inproj_kernel, perm=perm),
        grid=(ncb + 1, steps),
        in_specs=in_specs,
        out_specs=out_spec,
        out_shape=jax.ShapeDtypeStruct(out_shape, BF16),
        scratch_shapes=scratch,
        compiler_params=_params("arbitrary", "arbitrary"),
        name="in_proj_p%d" % perm,
    )(h, w, colscale)
    return out.reshape(b, s, ncols)


def _head_epilogue(o, g, zg):
    ms = jnp.mean(o * o, axis=-1, keepdims=True)
    return o * lax.rsqrt(ms + EPS) * g * _silu(zg.astype(F32))


def _sb_kernel(q_ref, k_ref, v_ref, z_ref, g_ref, o_ref, acc_ref, carry_ref, *, seq, tile):
    row = lax.broadcasted_iota(jnp.int32, (tile, tile), 0)
    col = lax.broadcasted_iota(jnp.int32, (tile, tile), 1)
    causal = col < row
    upper = jnp.where(row > col, 1.0, 0.0).astype(BF16)

    def key_tile(q, k, v, carry, mask):
        s = _dot_nt(q, k)
        e = jnp.exp2(-jnp.abs(s))
        lb = jnp.minimum(s, 0.0) - jnp.log(1.0 + e) * LOG2E
        lom = lb - s
        if mask is not None:
            lom = jnp.where(mask, lom, 0.0)
        arg = lb + jnp.dot(lom.astype(BF16), upper, preferred_element_type=F32)
        if carry is not None:
            arg = arg + carry
        a = jnp.exp2(arg)
        if mask is not None:
            a = jnp.where(mask, a, 0.0)
        av = jnp.dot(a.astype(BF16), v, preferred_element_type=F32)
        return av, jnp.sum(lom, axis=1, keepdims=True)

    for kj in reversed(range(seq // tile)):
        k0 = kj * tile
        k = k_ref[pl.ds(k0, tile), :]
        v = v_ref[pl.ds(k0, tile), :]
        av, rsum = key_tile(q_ref[pl.ds(k0, tile), :], k, v, None, causal)
        acc_ref[pl.ds(k0, tile), :] = av
        carry_ref[pl.ds(k0, tile), :] = rsum
        if k0 + tile < seq:
            later = pl.ds(k0 + tile, seq - k0 - tile)
            av, rsum = key_tile(q_ref[later, :], k, v, carry_ref[later, :], None)
            acc_ref[later, :] += av
            carry_ref[later, :] += rsum

    for q0 in range(0, seq, tile):
        rows = pl.ds(q0, tile)
        o_ref[rows, :] = _head_epilogue(acc_ref[rows, :], g_ref[...], z_ref[rows, :]).astype(BF16)


def _head_spec(s, n_heads, off):
    return pl.BlockSpec((None, s, HEAD_DIM), lambda i, h: (i, 0, off * n_heads + h))


def _sb_attention(proj, g_sb, n_heads):
    b, s, _ = proj.shape
    tile = _pick(s, V7X_MXU_DIM)
    return pl.pallas_call(
        functools.partial(_sb_kernel, seq=s, tile=tile),
        grid=(b, n_heads),
        in_specs=[_head_spec(s, n_heads, 0), _head_spec(s, n_heads, 1), _head_spec(s, n_heads, 2),
                  _head_spec(s, n_heads, 3), pl.BlockSpec((1, HEAD_DIM), lambda i, h: (0, h))],
        out_specs=pl.BlockSpec((None, s, HEAD_DIM), lambda i, h: (i, 0, h)),
        out_shape=jax.ShapeDtypeStruct((b, s, n_heads * HEAD_DIM), BF16),
        scratch_shapes=[pltpu.VMEM((s, HEAD_DIM), F32), pltpu.VMEM((s, 1), F32)],
        compiler_params=_params("arbitrary", "arbitrary"),
        name="sb_attn",
    )(proj, proj, proj, proj, g_sb)


def _dil_kernel(q_ref, k_ref, v_ref, z_ref, g_ref, slope_ref, onat,
                qf, kf, vf, q16, k16, vo, vo16, num_s, den_s, max_s, *, seq):
    n = DIL_BLOCK
    seg = seq // DIL_PERM
    sub = n // DIL_PERM
    slope = slope_ref[0:1, 0:1]

    def bias_of(dist, window):
        return jnp.where((dist >= 0) & (dist <= window), dist.astype(F32) * (-slope), MASK_BIAS)

    def iota2(nk):
        return (lax.broadcasted_iota(jnp.int32, (n, nk), 0), lax.broadcasted_iota(jnp.int32, (n, nk), 1))

    def attend(q, k, vv, bias):
        s = _dot_nt(q, k) + bias
        m = jnp.max(s, axis=1, keepdims=True)
        p = jnp.exp2(s - m)
        pv = jnp.dot(p.astype(BF16), vv, preferred_element_type=F32)
        return pv[:, :HEAD_DIM], pv[:, HEAD_DIM:], jnp.broadcast_to(m, (n, HEAD_DIM))

    def put(br, idx, res):
        num_s[br, idx, :], den_s[br, idx, :], max_s[br, idx, :] = res

    ones = jnp.ones((seq, HEAD_DIM), BF16)
    vo[:, :HEAD_DIM] = v_ref[...]
    vo[:, HEAD_DIM:] = ones
    vo16[:, HEAD_DIM:] = ones

    a, c = iota2(2 * n)
    tq = (a % sub) * DIL_PERM + a // sub
    bias_prev = bias_of(tq - ((c % (2 * sub)) * DIL_PERM + c // (2 * sub) - n), DIL_PAIRS[0][0])
    a, c = iota2(n)
    bias_first = bias_of(tq[:, :n] - ((c % sub) * DIL_PERM + c // sub), DIL_PAIRS[0][0])

    def gather(ref, start, size):
        return jnp.concatenate([ref[pl.ds(res * seg + start, size), :] for res in range(DIL_PERM)], axis=0)

    for i in range(seq // n):
        q = gather(q_ref, sub * i, sub)
        if i == 0:
            res3 = attend(q, gather(k_ref, 0, sub), gather(vo, 0, sub), bias_first)
        else:
            res3 = attend(q, gather(k_ref, sub * (i - 1), 2 * sub), gather(vo, sub * (i - 1), 2 * sub), bias_prev)
        for res in range(DIL_PERM):
            put(0, pl.ds(res * seg + sub * i, sub), tuple(t[res * sub:(res + 1) * sub] for t in res3))

    r = DIL_PAIRS[1][1]
    assert r == DIL_PERM
    a, c = iota2(2 * n)
    bias_prev = bias_of(r * (a - c + n), DIL_PAIRS[1][0])
    bias_first = bias_prev[:, n:]
    r16 = DIL_PAIRS[2][1]
    step = r16 // DIL_PERM
    assert seq // r16 == n
    a, c = iota2(n)
    bias_16 = bias_of(r16 * (a - c), DIL_PAIRS[2][0])
    chunk = _pick(seg, ROW_CHUNK)

    for res in range(DIL_PERM):
        grp = pl.ds(res * seg, seg)
        for i in range(seg // n):
            row0 = res * seg + i * n
            q = q_ref[pl.ds(row0, n), :]
            if i == 0:
                res3 = attend(q, k_ref[pl.ds(row0, n), :], vo[pl.ds(row0, n), :], bias_first)
            else:
                res3 = attend(q, k_ref[pl.ds(row0 - n, 2 * n), :], vo[pl.ds(row0 - n, 2 * n), :], bias_prev)
            put(1, pl.ds(row0, n), res3)

        qf[grp, :] = q_ref[grp, :].astype(F32)
        kf[grp, :] = k_ref[grp, :].astype(F32)
        vf[grp, :] = v_ref[grp, :].astype(F32)
        for sres in range(step):
            src = pl.ds(res * seg + sres, n, stride=step)
            blk = pl.ds((res * step + sres) * n, n)
            q16[blk, :] = qf[src, :].astype(BF16)
            k16[blk, :] = kf[src, :].astype(BF16)
            vo16[blk, :HEAD_DIM] = vf[src, :].astype(BF16)
            put(2, src, attend(q16[blk, :], k16[blk, :], vo16[blk, :], bias_16))

        for l0 in range(0, seg, chunk):
            rows = pl.ds(res * seg + l0, chunk)
            m_all = jnp.maximum(jnp.maximum(max_s[0, rows, :], max_s[1, rows, :]), max_s[2, rows, :])
            num = jnp.zeros((chunk, HEAD_DIM), F32)
            den = jnp.zeros((chunk, HEAD_DIM), F32)
            for br in range(len(DIL_PAIRS)):
                w = jnp.exp2(max_s[br, rows, :] - m_all)
                num = num + num_s[br, rows, :] * w
                den = den + den_s[br, rows, :] * w
            onat[pl.ds(DIL_PERM * l0 + res, chunk, stride=DIL_PERM), :] = _head_epilogue(
                num / den, g_ref[...], z_ref[rows, :])


def _dil_group_kernel(*refs, seq, heads):
    per_head = 6
    o_ref, onat = refs[per_head * heads:per_head * heads + 2]
    scratch = refs[per_head * heads + 2:]
    n_scr = len(scratch) // heads
    for hh in range(heads):
        ins = refs[per_head * hh:per_head * (hh + 1)]
        _dil_kernel(*ins, onat.at[hh], *scratch[n_scr * hh:n_scr * (hh + 1)], seq=seq)
    chunk = _pick(seq, ROW_CHUNK)
    for c0 in range(0, seq, chunk):
        rows = pl.ds(c0, chunk)
        o_ref[rows, :] = jnp.concatenate([onat[hh, rows, :] for hh in range(heads)], axis=1).astype(BF16)


def _dil_attention(proj, g_dil, slopes, n_heads):
    b, s, _ = proj.shape
    nbr = len(DIL_PAIRS)
    heads = DIL_HEADS_PER_STEP
    width = heads * HEAD_DIM
    groups = n_heads // heads
    assert n_heads % heads == 0
    rows_f32 = pltpu.VMEM((s, HEAD_DIM), F32)
    rows_bf16 = pltpu.VMEM((s, HEAD_DIM), BF16)
    rows2_bf16 = pltpu.VMEM((s, 2 * HEAD_DIM), BF16)
    per_branch = pltpu.VMEM((nbr, s, HEAD_DIM), F32)

    in_specs, operands = [], []
    for hh in range(heads):
        def head(h, hh=hh):
            return heads * h + hh

        for off in range(4):
            in_specs.append(pl.BlockSpec((None, s, HEAD_DIM),
                                         lambda i, h, off=off, head=head: (i, 0, off * n_heads + head(h))))
        in_specs.append(pl.BlockSpec((1, HEAD_DIM), lambda i, h, head=head: (0, head(h))))
        in_specs.append(pl.BlockSpec((None, 1, HEAD_DIM), lambda i, h, head=head: (head(h), 0, 0)))
        operands += [proj, proj, proj, proj, g_dil, slopes]

    return pl.pallas_call(
        functools.partial(_dil_group_kernel, seq=s, heads=heads),
        grid=(b, groups),
        in_specs=in_specs,
        out_specs=pl.BlockSpec((None, s, width), lambda i, h: (i, 0, h)),
        out_shape=jax.ShapeDtypeStruct((b, s, n_heads * HEAD_DIM), BF16),
        scratch_shapes=[pltpu.VMEM((heads, s, HEAD_DIM), F32)]
        + heads * [rows_f32, rows_f32, rows_f32, rows_bf16, rows_bf16, rows2_bf16, rows2_bf16,
                   per_branch, per_branch, per_branch],
        compiler_params=_params("arbitrary", "arbitrary"),
        name="dil_attn",
    )(*operands)


def _outproj_kernel(ya_ref, yb_ref, wk_ref, x_ref, gate_ref, o_ref, wb_ref):
    j = pl.program_id(0)
    i = pl.program_id(1)
    kc = wk_ref.shape[0]
    ka = ya_ref.shape[1]

    def cast_chunk():
        wb_ref[j % 2, pl.ds(pl.multiple_of(i * kc, kc), kc), :] = wk_ref[...].astype(BF16)

    @pl.when(j == 0)
    def _():
        cast_chunk()

    @pl.when(j > 0)
    def _():
        cast_chunk()
        cur = (j - 1) % 2
        acc = jnp.dot(ya_ref[...], wb_ref[cur, :ka, :], preferred_element_type=F32)
        acc = acc + jnp.dot(yb_ref[...], wb_ref[cur, ka:, :], preferred_element_type=F32)
        o_ref[...] = x_ref[...] + gate_ref[...] * acc


def _outproj(y_sb, y_dl, w_out, x, gate):
    b, s, d = x.shape
    ka = y_sb.shape[2]
    kb = y_dl.shape[2]
    tm = _pick(s, OUT_PROJ_TM)
    tn = _pick(d, OUT_PROJ_TN)
    nm = s // tm
    steps = b * nm
    ncb = d // tn
    kc = (ka + kb) // steps
    assert (ka + kb) % steps == 0 and kc % BF16_ROWS_PER_VREG == 0

    def row(j, i):
        i = jnp.where(j == 0, 0, i)
        return i // nm, i % nm

    def col(j):
        return jnp.maximum(j - 1, 0)

    return pl.pallas_call(
        _outproj_kernel,
        grid=(ncb + 1, steps),
        in_specs=[
            pl.BlockSpec((None, tm, ka), lambda j, i: (*row(j, i), 0)),
            pl.BlockSpec((None, tm, kb), lambda j, i: (*row(j, i), 0)),
            pl.BlockSpec((kc, tn), lambda j, i: (i, jnp.minimum(j, ncb - 1))),
            pl.BlockSpec((None, tm, tn), lambda j, i: (*row(j, i), col(j))),
            pl.BlockSpec((None, 1, tn), lambda j, i: (row(j, i)[0], 0, col(j))),
        ],
        out_specs=pl.BlockSpec((None, tm, tn), lambda j, i: (*row(j, i), col(j))),
        out_shape=jax.ShapeDtypeStruct((b, s, d), F32),
        scratch_shapes=[pltpu.VMEM((2, ka + kb, tn), BF16)],
        compiler_params=pltpu.CompilerParams(dimension_semantics=("arbitrary", "arbitrary"),
                                             vmem_limit_bytes=OUT_PROJ_VMEM_LIMIT),
        name="out_proj",
    )(y_sb, y_dl, w_out, x, gate)


def _final_norm_kernel(x_ref, g_ref, o_ref):
    x = x_ref[...]
    ms = jnp.mean(x * x, axis=-1, keepdims=True)
    o_ref[...] = x * lax.rsqrt(ms + EPS) * g_ref[...]


def _final_norm(x, g):
    b, s, d = x.shape
    ts = _pick(s, NORM_ROWS)
    return pl.pallas_call(
        _final_norm_kernel,
        grid=(b, s // ts),
        in_specs=[pl.BlockSpec((None, ts, d), lambda i, j: (i, j, 0)),
                  pl.BlockSpec((1, d), lambda i, j: (0, 0))],
        out_specs=pl.BlockSpec((None, ts, d), lambda i, j: (i, j, 0)),
        out_shape=jax.ShapeDtypeStruct((b, s, d), F32),
        compiler_params=_params("arbitrary", "arbitrary"),
        name="final_norm",
    )(x, g)


def kernel(x, c, w_ada, b_ada, g_norm, w_in, g_sb, g_dil, w_out, g_final):
    b, s, d = x.shape
    depth = w_ada.shape[0]
    n_sb = g_sb.shape[1] // HEAD_DIM
    n_dil = g_dil.shape[1] // HEAD_DIM
    sb_w = n_sb * HEAD_DIM
    dil_w = n_dil * HEAD_DIM
    assert s % DIL_PAIRS[-1][0] == 0 and sb_w == dil_w

    qscale = LOG2E / math.sqrt(HEAD_DIM)
    colscale = jnp.concatenate([
        jnp.full((sb_w,), qscale, F32), jnp.ones((3 * sb_w,), F32),
        jnp.full((dil_w,), qscale, F32), jnp.ones((3 * dil_w,), F32)])[None, :]
    slopes = jnp.exp2(-ALIBI_MAX_BIAS * jnp.arange(1, n_dil + 1, dtype=F32) / n_dil) * LOG2E
    slopes = jnp.broadcast_to(slopes[:, None, None], (n_dil, 1, HEAD_DIM))

    assert b <= BF16_ROWS_PER_VREG
    c_pad = jnp.zeros((BF16_ROWS_PER_VREG, d), F32).at[:b].set(c.astype(F32))
    for layer in range(depth):
        mod = _ada(c_pad, w_ada[layer], b_ada[layer][None, :])[:b]
        shift, scale, gate = (mod[:, i * d:(i + 1) * d][:, None, :] for i in range(3))
        h = _norm_mod(x, g_norm[layer][None, :], scale, shift)
        proj_sb = _inproj(h, w_in[layer], colscale, 0, 4 * sb_w, 1)
        proj_dl = _inproj(h, w_in[layer], colscale, 4 * sb_w, 4 * dil_w, DIL_PERM)
        y_sb = _sb_attention(proj_sb, g_sb[layer][None, :], n_sb)
        y_dl = _dil_attention(proj_dl, g_dil[layer][None, :], slopes, n_dil)
        x = _outproj(y_sb, y_dl, w_out[layer], x, gate)
    return _final_norm(x, g_final[None, :])
```

```python
import functools
import math

import jax
import jax.numpy as jnp
from jax import lax
from jax.experimental import pallas as pl
from jax.experimental.pallas import tpu as pltpu

HEAD_DIM = 128
EPS = 1e-6
LOG2E = 1.4426950408889634
ALIBI_MAX_BIAS = 8.0
DIL_PAIRS = ((128, 1), (512, 4), (2048, 16))
DIL_BLOCK = 128
DIL_PERM = 4
MASK_BIAS = -1e30
V7X_VMEM_BYTES = 64 * 1024 * 1024
V7X_MXU_DIM = 256
BF16_ROWS_PER_VREG = 16
VMEM_LIMIT = V7X_VMEM_BYTES - 8 * 1024 * 1024
IN_PROJ_TM, IN_PROJ_TN = 1024, 1024
OUT_PROJ_TM, OUT_PROJ_TN = 1024, 1024
OUT_PROJ_VMEM_LIMIT = V7X_VMEM_BYTES - 2 * 1024 * 1024
ADA_TN = 512
NORM_ROWS = 512
ROW_CHUNK = V7X_MXU_DIM
DIL_HEADS_PER_STEP = 2

F32 = jnp.float32
BF16 = jnp.bfloat16


def _params(*sem):
    return pltpu.CompilerParams(dimension_semantics=sem, vmem_limit_bytes=VMEM_LIMIT)


def _pick(n, pref):
    t = min(pref, n)
    while n % t:
        t //= 2
    return t


def _silu(x):
    return x / (1.0 + jnp.exp(-x))


def _dot_nt(a, b):
    return lax.dot_general(a, b, (((1,), (1,)), ((), ())), preferred_element_type=F32)


def _ada_kernel(c_ref, w_ref, b_ref, o_ref):
    cs = _silu(c_ref[...])
    acc = jnp.dot(cs.astype(BF16), w_ref[...].astype(BF16), preferred_element_type=F32)
    o_ref[...] = acc + b_ref[...]


def _ada(c_pad, w, b):
    m, d = c_pad.shape
    n = w.shape[1]
    tn = _pick(n, ADA_TN)
    return pl.pallas_call(
        _ada_kernel,
        grid=(n // tn,),
        in_specs=[
            pl.BlockSpec((m, d), lambda j: (0, 0)),
            pl.BlockSpec((d, tn), lambda j: (0, j)),
            pl.BlockSpec((1, tn), lambda j: (0, j)),
        ],
        out_specs=pl.BlockSpec((m, tn), lambda j: (0, j)),
        out_shape=jax.ShapeDtypeStruct((m, n), F32),
        compiler_params=_params("arbitrary"),
        name="ada_ln",
    )(c_pad, w, b)


def _norm_mod_kernel(x_ref, g_ref, scale_ref, shift_ref, h_ref, gs_ref):
    @pl.when(pl.program_id(1) == 0)
    def _():
        gs_ref[...] = g_ref[...] * (1.0 + scale_ref[...])

    x = x_ref[...]
    ms = jnp.mean(x * x, axis=-1, keepdims=True)
    h_ref[...] = (x * lax.rsqrt(ms + EPS) * gs_ref[...] + shift_ref[...]).astype(BF16)


def _norm_mod(x, g, scale, shift):
    b, s, d = x.shape
    ts = _pick(s, NORM_ROWS)
    return pl.pallas_call(
        _norm_mod_kernel,
        grid=(b, s // ts),
        in_specs=[
            pl.BlockSpec((None, ts, d), lambda i, j: (i, j, 0)),
            pl.BlockSpec((1, d), lambda i, j: (0, 0)),
            pl.BlockSpec((None, 1, d), lambda i, j: (i, 0, 0)),
            pl.BlockSpec((None, 1, d), lambda i, j: (i, 0, 0)),
        ],
        out_specs=pl.BlockSpec((None, ts, d), lambda i, j: (i, j, 0)),
        out_shape=jax.ShapeDtypeStruct((b, s, d), BF16),
        scratch_shapes=[pltpu.VMEM((1, d), F32)],
        compiler_params=_params("arbitrary", "arbitrary"),
        name="norm_mod",
    )(x, g, scale, shift)


def _inproj_kernel(h_ref, wk_ref, cs_ref, o_ref, wb_ref, *maybe_slabs, perm):
    j = pl.program_id(0)
    i = pl.program_id(1)
    kc = wk_ref.shape[0]

    def cast_chunk():
        rows = pl.ds(pl.multiple_of(i * kc, kc), kc)
        wb_ref[j % 2, rows, :] = (wk_ref[...] * cs_ref[...]).astype(BF16)

    @pl.when(j == 0)
    def _():
        cast_chunk()

    @pl.when(j > 0)
    def _():
        cast_chunk()
        vals = jnp.dot(h_ref[...], wb_ref[(j - 1) % 2], preferred_element_type=F32)
        if perm == 1:
            o_ref[...] = vals.astype(BF16)
        else:
            (slab_ref,) = maybe_slabs
            tm, tn = vals.shape
            lanes = slab_ref.shape[2]
            for slab in range(tn // lanes):
                slab_ref[slab] = vals[:, slab * lanes:(slab + 1) * lanes]
            for res in range(perm):
                for slab in range(tn // lanes):
                    rows = slab_ref[slab, pl.ds(res, tm // perm, stride=perm), :]
                    o_ref[res, :, slab * lanes:(slab + 1) * lanes] = rows.astype(BF16)


def _inproj(h, w, colscale, col0, ncols, perm):
    b, s, d = h.shape
    tm = _pick(s, IN_PROJ_TM)
    tn = _pick(ncols, IN_PROJ_TN)
    nm = s // tm
    steps = b * nm
    ncb = ncols // tn
    joff = col0 // tn
    kc = d // steps
    assert col0 % tn == 0 and tm % (BF16_ROWS_PER_VREG * perm) == 0
    assert d % steps == 0 and kc % BF16_ROWS_PER_VREG == 0

    def row(j, i):
        i = jnp.where(j == 0, 0, i)
        return i // nm, i % nm

    def next_col(j):
        return joff + jnp.minimum(j, ncb - 1)

    in_specs = [
        pl.BlockSpec((None, tm, d), lambda j, i: (*row(j, i), 0)),
        pl.BlockSpec((kc, tn), lambda j, i: (i, next_col(j))),
        pl.BlockSpec((1, tn), lambda j, i: (0, next_col(j))),
    ]
    scratch = [pltpu.VMEM((2, d, tn), BF16)]
    if perm == 1:
        out_spec = pl.BlockSpec((None, tm, tn), lambda j, i: (*row(j, i), jnp.maximum(j - 1, 0)))
        out_shape = (b, s, ncols)
    else:
        out_spec = pl.BlockSpec((None, perm, tm // perm, tn),
                                lambda j, i: (row(j, i)[0], 0, row(j, i)[1], jnp.maximum(j - 1, 0)))
        out_shape = (b, perm, s // perm, ncols)
        scratch.append(pltpu.VMEM((tn // HEAD_DIM, tm, HEAD_DIM), F32))
    out = pl.pallas_call(
        functools.partial(_inproj_kernel, perm=perm),
        grid=(ncb + 1, steps),
        in_specs=in_specs,
        out_specs=out_spec,
        out_shape=jax.ShapeDtypeStruct(out_shape, BF16),
        scratch_shapes=scratch,
        compiler_params=_params("arbitrary", "arbitrary"),
        name="in_proj_p%d" % perm,
    )(h, w, colscale)
    return out.reshape(b, s, ncols)


def _head_epilogue(o, g, zg):
    ms = jnp.mean(o * o, axis=-1, keepdims=True)
    return o * lax.rsqrt(ms + EPS) * g * _silu(zg.astype(F32))


def _sb_kernel(q_ref, k_ref, v_ref, z_ref, g_ref, o_ref, acc_ref, carry_ref, lbc_all, lom_all,
               *, seq, tile):
    row = lax.broadcasted_iota(jnp.int32, (tile, tile), 0)
    col = lax.broadcasted_iota(jnp.int32, (tile, tile), 1)
    causal = col < row
    upper = jnp.where(row > col, 1.0, 0.0).astype(BF16)

    def logs(kj, rows, carry, mask):
        s = _dot_nt(q_ref[rows, :], k_ref[pl.ds(kj * tile, tile), :])
        e = jnp.exp2(-jnp.abs(s))
        lb = jnp.minimum(s, 0.0) - jnp.log(1.0 + e) * LOG2E
        lom = lb - s
        if mask is not None:
            lom = jnp.where(mask, lom, 0.0)
        lom_all[kj, rows, :] = lom.astype(BF16)
        lbc_all[kj, rows, :] = lb if carry is None else lb + carry
        return jnp.sum(lom, axis=1, keepdims=True)

    def weights_times_v(kj, rows, mask):
        arg = lbc_all[kj, rows, :] + jnp.dot(lom_all[kj, rows, :], upper, preferred_element_type=F32)
        a = jnp.exp2(arg)
        if mask is not None:
            a = jnp.where(mask, a, 0.0)
        return jnp.dot(a.astype(BF16), v_ref[pl.ds(kj * tile, tile), :], preferred_element_type=F32)

    def pass1(kj):
        k0 = kj * tile
        first = pl.ds(k0, tile)
        carry_ref[first, :] = logs(kj, first, None, causal)
        if k0 + tile < seq:
            later = pl.ds(k0 + tile, seq - k0 - tile)
            carry_ref[later, :] += logs(kj, later, carry_ref[later, :], None)

    def pass2(kj):
        k0 = kj * tile
        first = pl.ds(k0, tile)
        acc_ref[first, :] = weights_times_v(kj, first, causal)
        if k0 + tile < seq:
            later = pl.ds(k0 + tile, seq - k0 - tile)
            acc_ref[later, :] += weights_times_v(kj, later, None)

    for kj in reversed(range(seq // tile)):
        pass1(kj)
    for kj in reversed(range(seq // tile)):
        pass2(kj)

    for q0 in range(0, seq, tile):
        rows = pl.ds(q0, tile)
        o_ref[rows, :] = _head_epilogue(acc_ref[rows, :], g_ref[...], z_ref[rows, :]).astype(BF16)


def _head_spec(s, n_heads, off):
    return pl.BlockSpec((None, s, HEAD_DIM), lambda i, h: (i, 0, off * n_heads + h))


def _sb_attention(proj, g_sb, n_heads):
    b, s, _ = proj.shape
    tile = _pick(s, V7X_MXU_DIM)
    return pl.pallas_call(
        functools.partial(_sb_kernel, seq=s, tile=tile),
        grid=(b, n_heads),
        in_specs=[_head_spec(s, n_heads, 0), _head_spec(s, n_heads, 1), _head_spec(s, n_heads, 2),
                  _head_spec(s, n_heads, 3), pl.BlockSpec((1, HEAD_DIM), lambda i, h: (0, h))],
        out_specs=pl.BlockSpec((None, s, HEAD_DIM), lambda i, h: (i, 0, h)),
        out_shape=jax.ShapeDtypeStruct((b, s, n_heads * HEAD_DIM), BF16),
        scratch_shapes=[pltpu.VMEM((s, HEAD_DIM), F32), pltpu.VMEM((s, 1), F32),
                        pltpu.VMEM((s // tile, s, tile), F32), pltpu.VMEM((s // tile, s, tile), BF16)],
        compiler_params=_params("arbitrary", "arbitrary"),
        name="sb_attn",
    )(proj, proj, proj, proj, g_sb)


def _dil_kernel(q_ref, k_ref, v_ref, z_ref, g_ref, slope_ref, onat,
                qf, kf, vf, q16, k16, vo, vo16, num_s, den_s, max_s, *, seq):
    n = DIL_BLOCK
    seg = seq // DIL_PERM
    sub = n // DIL_PERM
    slope = slope_ref[0:1, 0:1]

    def bias_of(dist, window):
        return jnp.where((dist >= 0) & (dist <= window), dist.astype(F32) * (-slope), MASK_BIAS)

    def iota2(nk):
        return (lax.broadcasted_iota(jnp.int32, (n, nk), 0), lax.broadcasted_iota(jnp.int32, (n, nk), 1))

    def attend(q, k, vv, bias):
        s = _dot_nt(q, k) + bias
        m = jnp.max(s, axis=1, keepdims=True)
        p = jnp.exp2(s - m)
        pv = jnp.dot(p.astype(BF16), vv, preferred_element_type=F32)
        return pv[:, :HEAD_DIM], pv[:, HEAD_DIM:], jnp.broadcast_to(m, (n, HEAD_DIM))

    def put(br, idx, res):
        num_s[br, idx, :], den_s[br, idx, :], max_s[br, idx, :] = res

    ones = jnp.ones((seq, HEAD_DIM), BF16)
    vo[:, :HEAD_DIM] = v_ref[...]
    vo[:, HEAD_DIM:] = ones
    vo16[:, HEAD_DIM:] = ones

    a, c = iota2(2 * n)
    tq = (a % sub) * DIL_PERM + a // sub
    bias_prev = bias_of(tq - ((c % (2 * sub)) * DIL_PERM + c // (2 * sub) - n), DIL_PAIRS[0][0])
    a, c = iota2(n)
    bias_first = bias_of(tq[:, :n] - ((c % sub) * DIL_PERM + c // sub), DIL_PAIRS[0][0])

    def gather(ref, start, size):
        return jnp.concatenate([ref[pl.ds(res * seg + start, size), :] for res in range(DIL_PERM)], axis=0)

    for i in range(seq // n):
        q = gather(q_ref, sub * i, sub)
        if i == 0:
            res3 = attend(q, gather(k_ref, 0, sub), gather(vo, 0, sub), bias_first)
        else:
            res3 = attend(q, gather(k_ref, sub * (i - 1), 2 * sub), gather(vo, sub * (i - 1), 2 * sub), bias_prev)
        for res in range(DIL_PERM):
            put(0, pl.ds(res * seg + sub * i, sub), tuple(t[res * sub:(res + 1) * sub] for t in res3))

    r = DIL_PAIRS[1][1]
    assert r == DIL_PERM
    a, c = iota2(2 * n)
    bias_prev = bias_of(r * (a - c + n), DIL_PAIRS[1][0])
    bias_first = bias_prev[:, n:]
    r16 = DIL_PAIRS[2][1]
    step = r16 // DIL_PERM
    assert seq // r16 == n
    a, c = iota2(n)
    bias_16 = bias_of(r16 * (a - c), DIL_PAIRS[2][0])
    chunk = _pick(seg, ROW_CHUNK)

    for res in range(DIL_PERM):
        grp = pl.ds(res * seg, seg)
        for i in range(seg // n):
            row0 = res * seg + i * n
            q = q_ref[pl.ds(row0, n), :]
            if i == 0:
                res3 = attend(q, k_ref[pl.ds(row0, n), :], vo[pl.ds(row0, n), :], bias_first)
            else:
                res3 = attend(q, k_ref[pl.ds(row0 - n, 2 * n), :], vo[pl.ds(row0 - n, 2 * n), :], bias_prev)
            put(1, pl.ds(row0, n), res3)

        qf[grp, :] = q_ref[grp, :].astype(F32)
        kf[grp, :] = k_ref[grp, :].astype(F32)
        vf[grp, :] = v_ref[grp, :].astype(F32)
        for sres in range(step):
            src = pl.ds(res * seg + sres, n, stride=step)
            blk = pl.ds((res * step + sres) * n, n)
            q16[blk, :] = qf[src, :].astype(BF16)
            k16[blk, :] = kf[src, :].astype(BF16)
            vo16[blk, :HEAD_DIM] = vf[src, :].astype(BF16)
            put(2, src, attend(q16[blk, :], k16[blk, :], vo16[blk, :], bias_16))

        for l0 in range(0, seg, chunk):
            rows = pl.ds(res * seg + l0, chunk)
            m_all = jnp.maximum(jnp.maximum(max_s[0, rows, :], max_s[1, rows, :]), max_s[2, rows, :])
            num = jnp.zeros((chunk, HEAD_DIM), F32)
            den = jnp.zeros((chunk, HEAD_DIM), F32)
            for br in range(len(DIL_PAIRS)):
                w = jnp.exp2(max_s[br, rows, :] - m_all)
                num = num + num_s[br, rows, :] * w
                den = den + den_s[br, rows, :] * w
            onat[pl.ds(DIL_PERM * l0 + res, chunk, stride=DIL_PERM), :] = _head_epilogue(
                num / den, g_ref[...], z_ref[rows, :])


def _dil_group_kernel(*refs, seq, heads):
    per_head = 6
    o_ref, onat = refs[per_head * heads:per_head * heads + 2]
    scratch = refs[per_head * heads + 2:]
    n_scr = len(scratch) // heads
    for hh in range(heads):
        ins = refs[per_head * hh:per_head * (hh + 1)]
        _dil_kernel(*ins, onat.at[hh], *scratch[n_scr * hh:n_scr * (hh + 1)], seq=seq)
    chunk = _pick(seq, ROW_CHUNK)
    for c0 in range(0, seq, chunk):
        rows = pl.ds(c0, chunk)
        o_ref[rows, :] = jnp.concatenate([onat[hh, rows, :] for hh in range(heads)], axis=1).astype(BF16)


def _dil_attention(proj, g_dil, slopes, n_heads):
    b, s, _ = proj.shape
    nbr = len(DIL_PAIRS)
    heads = DIL_HEADS_PER_STEP
    width = heads * HEAD_DIM
    groups = n_heads // heads
    assert n_heads % heads == 0
    rows_f32 = pltpu.VMEM((s, HEAD_DIM), F32)
    rows_bf16 = pltpu.VMEM((s, HEAD_DIM), BF16)
    rows2_bf16 = pltpu.VMEM((s, 2 * HEAD_DIM), BF16)
    per_branch = pltpu.VMEM((nbr, s, HEAD_DIM), F32)

    in_specs, operands = [], []
    for hh in range(heads):
        def head(h, hh=hh):
            return heads * h + hh

        for off in range(4):
            in_specs.append(pl.BlockSpec((None, s, HEAD_DIM),
                                         lambda i, h, off=off, head=head: (i, 0, off * n_heads + head(h))))
        in_specs.append(pl.BlockSpec((1, HEAD_DIM), lambda i, h, head=head: (0, head(h))))
        in_specs.append(pl.BlockSpec((None, 1, HEAD_DIM), lambda i, h, head=head: (head(h), 0, 0)))
        operands += [proj, proj, proj, proj, g_dil, slopes]

    return pl.pallas_call(
        functools.partial(_dil_group_kernel, seq=s, heads=heads),
        grid=(b, groups),
        in_specs=in_specs,
        out_specs=pl.BlockSpec((None, s, width), lambda i, h: (i, 0, h)),
        out_shape=jax.ShapeDtypeStruct((b, s, n_heads * HEAD_DIM), BF16),
        scratch_shapes=[pltpu.VMEM((heads, s, HEAD_DIM), F32)]
        + heads * [rows_f32, rows_f32, rows_f32, rows_bf16, rows_bf16, rows2_bf16, rows2_bf16,
                   per_branch, per_branch, per_branch],
        compiler_params=_params("arbitrary", "arbitrary"),
        name="dil_attn",
    )(*operands)


def _outproj_kernel(ya_ref, yb_ref, wk_ref, x_ref, gate_ref, o_ref, wb_ref):
    j = pl.program_id(0)
    i = pl.program_id(1)
    kc = wk_ref.shape[0]
    ka = ya_ref.shape[1]

    def cast_chunk():
        wb_ref[j % 2, pl.ds(pl.multiple_of(i * kc, kc), kc), :] = wk_ref[...].astype(BF16)

    @pl.when(j == 0)
    def _():
        cast_chunk()

    @pl.when(j > 0)
    def _():
        cast_chunk()
        cur = (j - 1) % 2
        acc = jnp.dot(ya_ref[...], wb_ref[cur, :ka, :], preferred_element_type=F32)
        acc = acc + jnp.dot(yb_ref[...], wb_ref[cur, ka:, :], preferred_element_type=F32)
        o_ref[...] = x_ref[...] + gate_ref[...] * acc


def _outproj(y_sb, y_dl, w_out, x, gate):
    b, s, d = x.shape
    ka = y_sb.shape[2]
    kb = y_dl.shape[2]
    tm = _pick(s, OUT_PROJ_TM)
    tn = _pick(d, OUT_PROJ_TN)
    nm = s // tm
    steps = b * nm
    ncb = d // tn
    kc = (ka + kb) // steps
    assert (ka + kb) % steps == 0 and kc % BF16_ROWS_PER_VREG == 0

    def row(j, i):
        i = jnp.where(j == 0, 0, i)
        return i // nm, i % nm

    def col(j):
        return jnp.maximum(j - 1, 0)

    return pl.pallas_call(
        _outproj_kernel,
        grid=(ncb + 1, steps),
        in_specs=[
            pl.BlockSpec((None, tm, ka), lambda j, i: (*row(j, i), 0)),
            pl.BlockSpec((None, tm, kb), lambda j, i: (*row(j, i), 0)),
            pl.BlockSpec((kc, tn), lambda j, i: (i, jnp.minimum(j, ncb - 1))),
            pl.BlockSpec((None, tm, tn), lambda j, i: (*row(j, i), col(j))),
            pl.BlockSpec((None, 1, tn), lambda j, i: (row(j, i)[0], 0, col(j))),
        ],
        out_specs=pl.BlockSpec((None, tm, tn), lambda j, i: (*row(j, i), col(j))),
        out_shape=jax.ShapeDtypeStruct((b, s, d), F32),
        scratch_shapes=[pltpu.VMEM((2, ka + kb, tn), BF16)],
        compiler_params=pltpu.CompilerParams(dimension_semantics=("arbitrary", "arbitrary"),
                                             vmem_limit_bytes=OUT_PROJ_VMEM_LIMIT),
        name="out_proj",
    )(y_sb, y_dl, w_out, x, gate)


def _final_norm_kernel(x_ref, g_ref, o_ref):
    x = x_ref[...]
    ms = jnp.mean(x * x, axis=-1, keepdims=True)
    o_ref[...] = x * lax.rsqrt(ms + EPS) * g_ref[...]


def _final_norm(x, g):
    b, s, d = x.shape
    ts = _pick(s, NORM_ROWS)
    return pl.pallas_call(
        _final_norm_kernel,
        grid=(b, s // ts),
        in_specs=[pl.BlockSpec((None, ts, d), lambda i, j: (i, j, 0)),
                  pl.BlockSpec((1, d), lambda i, j: (0, 0))],
        out_specs=pl.BlockSpec((None, ts, d), lambda i, j: (i, j, 0)),
        out_shape=jax.ShapeDtypeStruct((b, s, d), F32),
        compiler_params=_params("arbitrary", "arbitrary"),
        name="final_norm",
    )(x, g)


def kernel(x, c, w_ada, b_ada, g_norm, w_in, g_sb, g_dil, w_out, g_final):
    b, s, d = x.shape
    depth = w_ada.shape[0]
    n_sb = g_sb.shape[1] // HEAD_DIM
    n_dil = g_dil.shape[1] // HEAD_DIM
    sb_w = n_sb * HEAD_DIM
    dil_w = n_dil * HEAD_DIM
    assert s % DIL_PAIRS[-1][0] == 0 and sb_w == dil_w

    qscale = LOG2E / math.sqrt(HEAD_DIM)
    colscale = jnp.concatenate([
        jnp.full((sb_w,), qscale, F32), jnp.ones((3 * sb_w,), F32),
        jnp.full((dil_w,), qscale, F32), jnp.ones((3 * dil_w,), F32)])[None, :]
    slopes = jnp.exp2(-ALIBI_MAX_BIAS * jnp.arange(1, n_dil + 1, dtype=F32) / n_dil) * LOG2E
    slopes = jnp.broadcast_to(slopes[:, None, None], (n_dil, 1, HEAD_DIM))

    assert b <= BF16_ROWS_PER_VREG
    c_pad = jnp.zeros((BF16_ROWS_PER_VREG, d), F32).at[:b].set(c.astype(F32))
    for layer in range(depth):
        mod = _ada(c_pad, w_ada[layer], b_ada[layer][None, :])[:b]
        shift, scale, gate = (mod[:, i * d:(i + 1) * d][:, None, :] for i in range(3))
        h = _norm_mod(x, g_norm[layer][None, :], scale, shift)
        proj_sb = _inproj(h, w_in[layer], colscale, 0, 4 * sb_w, 1)
        proj_dl = _inproj(h, w_in[layer], colscale, 4 * sb_w, 4 * dil_w, DIL_PERM)
        y_sb = _sb_attention(proj_sb, g_sb[layer][None, :], n_sb)
        y_dl = _dil_attention(proj_dl, g_dil[layer][None, :], slopes, n_dil)
        x = _outproj(y_sb, y_dl, w_out[layer], x, gate)
    return _final_norm(x, g_final[None, :])
```

```python
import functools
import math

import jax
import jax.numpy as jnp
from jax import lax
from jax.experimental import pallas as pl
from jax.experimental.pallas import tpu as pltpu

HEAD_DIM = 128
EPS = 1e-6
LOG2E = 1.4426950408889634
ALIBI_MAX_BIAS = 8.0
DIL_PAIRS = ((128, 1), (512, 4), (2048, 16))
DIL_BLOCK = 128
DIL_PERM = 4
MASK_BIAS = -1e30
V7X_VMEM_BYTES = 64 * 1024 * 1024
V7X_MXU_DIM = 256
BF16_ROWS_PER_VREG = 16
VMEM_LIMIT = V7X_VMEM_BYTES - 8 * 1024 * 1024
IN_PROJ_TM, IN_PROJ_TN = 1024, 1024
OUT_PROJ_TM, OUT_PROJ_TN = 1024, 1024
OUT_PROJ_VMEM_LIMIT = V7X_VMEM_BYTES - 2 * 1024 * 1024
ADA_TN = 512
NORM_ROWS = 512
ROW_CHUNK = V7X_MXU_DIM
DIL_HEADS_PER_STEP = 2

F32 = jnp.float32
BF16 = jnp.bfloat16


def _params(*sem):
    return pltpu.CompilerParams(dimension_semantics=sem, vmem_limit_bytes=VMEM_LIMIT)


def _pick(n, pref):
    t = min(pref, n)
    while n % t:
        t //= 2
    return t


def _silu(x):
    return x / (1.0 + jnp.exp(-x))


def _dot_nt(a, b):
    return lax.dot_general(a, b, (((1,), (1,)), ((), ())), preferred_element_type=F32)


def _ada_kernel(c_ref, w_ref, b_ref, o_ref):
    cs = _silu(c_ref[...])
    acc = jnp.dot(cs.astype(BF16), w_ref[...].astype(BF16), preferred_element_type=F32)
    o_ref[...] = acc + b_ref[...]


def _ada(c_pad, w, b):
    m, d = c_pad.shape
    n = w.shape[1]
    tn = _pick(n, ADA_TN)
    return pl.pallas_call(
        _ada_kernel,
        grid=(n // tn,),
        in_specs=[
            pl.BlockSpec((m, d), lambda j: (0, 0)),
            pl.BlockSpec((d, tn), lambda j: (0, j)),
            pl.BlockSpec((1, tn), lambda j: (0, j)),
        ],
        out_specs=pl.BlockSpec((m, tn), lambda j: (0, j)),
        out_shape=jax.ShapeDtypeStruct((m, n), F32),
        compiler_params=_params("arbitrary"),
        name="ada_ln",
    )(c_pad, w, b)


def _norm_mod_kernel(x_ref, g_ref, scale_ref, shift_ref, h_ref, gs_ref):
    @pl.when(pl.program_id(1) == 0)
    def _():
        gs_ref[...] = g_ref[...] * (1.0 + scale_ref[...])

    x = x_ref[...]
    ms = jnp.mean(x * x, axis=-1, keepdims=True)
    h_ref[...] = (x * lax.rsqrt(ms + EPS) * gs_ref[...] + shift_ref[...]).astype(BF16)


def _norm_mod(x, g, scale, shift):
    b, s, d = x.shape
    ts = _pick(s, NORM_ROWS)
    return pl.pallas_call(
        _norm_mod_kernel,
        grid=(b, s // ts),
        in_specs=[
            pl.BlockSpec((None, ts, d), lambda i, j: (i, j, 0)),
            pl.BlockSpec((1, d), lambda i, j: (0, 0)),
            pl.BlockSpec((None, 1, d), lambda i, j: (i, 0, 0)),
            pl.BlockSpec((None, 1, d), lambda i, j: (i, 0, 0)),
        ],
        out_specs=pl.BlockSpec((None, ts, d), lambda i, j: (i, j, 0)),
        out_shape=jax.ShapeDtypeStruct((b, s, d), BF16),
        scratch_shapes=[pltpu.VMEM((1, d), F32)],
        compiler_params=_params("arbitrary", "arbitrary"),
        name="norm_mod",
    )(x, g, scale, shift)


def _inproj_kernel(h_ref, wk_ref, cs_ref, o_ref, wb_ref, *maybe_slabs, perm):
    j = pl.program_id(0)
    i = pl.program_id(1)
    kc = wk_ref.shape[0]

    def cast_chunk():
        rows = pl.ds(pl.multiple_of(i * kc, kc), kc)
        wb_ref[j % 2, rows, :] = (wk_ref[...] * cs_ref[...]).astype(BF16)

    @pl.when(j == 0)
    def _():
        cast_chunk()

    @pl.when(j > 0)
    def _():
        cast_chunk()
        vals = jnp.dot(h_ref[...], wb_ref[(j - 1) % 2], preferred_element_type=F32)
        if perm == 1:
            o_ref[...] = vals.astype(BF16)
        else:
            (slab_ref,) = maybe_slabs
            tm, tn = vals.shape
            lanes = slab_ref.shape[2]
            for slab in range(tn // lanes):
                slab_ref[slab] = vals[:, slab * lanes:(slab + 1) * lanes]
            for res in range(perm):
                for slab in range(tn // lanes):
                    rows = slab_ref[slab, pl.ds(res, tm // perm, stride=perm), :]
                    o_ref[res, :, slab * lanes:(slab + 1) * lanes] = rows.astype(BF16)


def _inproj(h, w, colscale, col0, ncols, perm):
    b, s, d = h.shape
    tm = _pick(s, IN_PROJ_TM)
    tn = _pick(ncols, IN_PROJ_TN)
    nm = s // tm
    steps = b * nm
    ncb = ncols // tn
    joff = col0 // tn
    kc = d // steps
    assert col0 % tn == 0 and tm % (BF16_ROWS_PER_VREG * perm) == 0
    assert d % steps == 0 and kc % BF16_ROWS_PER_VREG == 0

    def row(j, i):
        i = jnp.where(j == 0, 0, i)
        return i // nm, i % nm

    def next_col(j):
        return joff + jnp.minimum(j, ncb - 1)

    in_specs = [
        pl.BlockSpec((None, tm, d), lambda j, i: (*row(j, i), 0)),
        pl.BlockSpec((kc, tn), lambda j, i: (i, next_col(j))),
        pl.BlockSpec((1, tn), lambda j, i: (0, next_col(j))),
    ]
    scratch = [pltpu.VMEM((2, d, tn), BF16)]
    if perm == 1:
        out_spec = pl.BlockSpec((None, tm, tn), lambda j, i: (*row(j, i), jnp.maximum(j - 1, 0)))
        out_shape = (b, s, ncols)
    else:
        out_spec = pl.BlockSpec((None, perm, tm // perm, tn),
                                lambda j, i: (row(j, i)[0], 0, row(j, i)[1], jnp.maximum(j - 1, 0)))
        out_shape = (b, perm, s // perm, ncols)
        scratch.append(pltpu.VMEM((tn // HEAD_DIM, tm, HEAD_DIM), F32))
    out = pl.pallas_call(
        functools.partial(_inproj_kernel, perm=perm),
        grid=(ncb + 1, steps),
        in_specs=in_specs,
        out_specs=out_spec,
        out_shape=jax.ShapeDtypeStruct(out_shape, BF16),
        scratch_shapes=scratch,
        compiler_params=_params("arbitrary", "arbitrary"),
        name="in_proj_p%d" % perm,
    )(h, w, colscale)
    return out.reshape(b, s, ncols)


def _head_epilogue(o, g, zg):
    ms = jnp.mean(o * o, axis=-1, keepdims=True)
    return o * lax.rsqrt(ms + EPS) * g * _silu(zg.astype(F32))


def _sb_kernel(q_ref, k_ref, v_ref, z_ref, g_ref, o_ref, acc_ref, carry_ref, lbc_all, lom_all,
               *, seq, tile):
    row = lax.broadcasted_iota(jnp.int32, (tile, tile), 0)
    col = lax.broadcasted_iota(jnp.int32, (tile, tile), 1)
    causal = col < row
    upper = jnp.where(row > col, 1.0, 0.0).astype(BF16)

    def logs(kj, rows, carry, mask):
        s_full = _dot_nt(q_ref[rows, :], k_ref[pl.ds(kj * tile, tile), :])
        rsum = None
        for c0 in range(0, tile, HEAD_DIM):
            keys = pl.ds(c0, HEAD_DIM)
            s = s_full[:, c0:c0 + HEAD_DIM]
            e = jnp.exp2(-jnp.abs(s))
            lb = jnp.minimum(s, 0.0) - jnp.log(1.0 + e) * LOG2E
            lom = lb - s
            if mask is not None:
                lom = jnp.where(mask[:, c0:c0 + HEAD_DIM], lom, 0.0)
            lom_all[kj, rows, keys] = lom.astype(BF16)
            lbc_all[kj, rows, keys] = lb if carry is None else lb + carry
            part = jnp.sum(lom, axis=1, keepdims=True)
            rsum = part if rsum is None else rsum + part
        return rsum

    def weights_times_v(kj, rows, mask):
        arg = lbc_all[kj, rows, :] + jnp.dot(lom_all[kj, rows, :], upper, preferred_element_type=F32)
        a = jnp.exp2(arg)
        if mask is not None:
            a = jnp.where(mask, a, 0.0)
        return jnp.dot(a.astype(BF16), v_ref[pl.ds(kj * tile, tile), :], preferred_element_type=F32)

    def pass1(kj):
        k0 = kj * tile
        first = pl.ds(k0, tile)
        carry_ref[first, :] = logs(kj, first, None, causal)
        if k0 + tile < seq:
            later = pl.ds(k0 + tile, seq - k0 - tile)
            carry_ref[later, :] += logs(kj, later, carry_ref[later, :], None)

    def pass2(kj):
        k0 = kj * tile
        first = pl.ds(k0, tile)
        acc_ref[first, :] = weights_times_v(kj, first, causal)
        if k0 + tile < seq:
            later = pl.ds(k0 + tile, seq - k0 - tile)
            acc_ref[later, :] += weights_times_v(kj, later, None)

    for kj in reversed(range(seq // tile)):
        pass1(kj)
    for kj in reversed(range(seq // tile)):
        pass2(kj)

    for q0 in range(0, seq, tile):
        rows = pl.ds(q0, tile)
        o_ref[rows, :] = _head_epilogue(acc_ref[rows, :], g_ref[...], z_ref[rows, :]).astype(BF16)


def _head_spec(s, n_heads, off):
    return pl.BlockSpec((None, s, HEAD_DIM), lambda i, h: (i, 0, off * n_heads + h))


def _sb_attention(proj, g_sb, n_heads):
    b, s, _ = proj.shape
    tile = _pick(s, V7X_MXU_DIM)
    return pl.pallas_call(
        functools.partial(_sb_kernel, seq=s, tile=tile),
        grid=(b, n_heads),
        in_specs=[_head_spec(s, n_heads, 0), _head_spec(s, n_heads, 1), _head_spec(s, n_heads, 2),
                  _head_spec(s, n_heads, 3), pl.BlockSpec((1, HEAD_DIM), lambda i, h: (0, h))],
        out_specs=pl.BlockSpec((None, s, HEAD_DIM), lambda i, h: (i, 0, h)),
        out_shape=jax.ShapeDtypeStruct((b, s, n_heads * HEAD_DIM), BF16),
        scratch_shapes=[pltpu.VMEM((s, HEAD_DIM), F32), pltpu.VMEM((s, 1), F32),
                        pltpu.VMEM((s // tile, s, tile), F32), pltpu.VMEM((s // tile, s, tile), BF16)],
        compiler_params=_params("arbitrary", "arbitrary"),
        name="sb_attn",
    )(proj, proj, proj, proj, g_sb)


def _dil_kernel(q_ref, k_ref, v_ref, z_ref, g_ref, slope_ref, onat,
                qf, kf, vf, q16, k16, vo, vo16, num_s, den_s, max_s, *, seq):
    n = DIL_BLOCK
    seg = seq // DIL_PERM
    sub = n // DIL_PERM
    slope = slope_ref[0:1, 0:1]

    def bias_of(dist, window):
        return jnp.where((dist >= 0) & (dist <= window), dist.astype(F32) * (-slope), MASK_BIAS)

    def iota2(nk):
        return (lax.broadcasted_iota(jnp.int32, (n, nk), 0), lax.broadcasted_iota(jnp.int32, (n, nk), 1))

    def attend(q, k, vv, bias):
        s = _dot_nt(q, k) + bias
        m = jnp.max(s, axis=1, keepdims=True)
        p = jnp.exp2(s - m)
        pv = jnp.dot(p.astype(BF16), vv, preferred_element_type=F32)
        return pv[:, :HEAD_DIM], pv[:, HEAD_DIM:], jnp.broadcast_to(m, (n, HEAD_DIM))

    def put(br, idx, res):
        num_s[br, idx, :], den_s[br, idx, :], max_s[br, idx, :] = res

    ones = jnp.ones((seq, HEAD_DIM), BF16)
    vo[:, :HEAD_DIM] = v_ref[...]
    vo[:, HEAD_DIM:] = ones
    vo16[:, HEAD_DIM:] = ones

    a, c = iota2(2 * n)
    tq = (a % sub) * DIL_PERM + a // sub
    bias_prev = bias_of(tq - ((c % (2 * sub)) * DIL_PERM + c // (2 * sub) - n), DIL_PAIRS[0][0])
    a, c = iota2(n)
    bias_first = bias_of(tq[:, :n] - ((c % sub) * DIL_PERM + c // sub), DIL_PAIRS[0][0])

    def gather(ref, start, size):
        return jnp.concatenate([ref[pl.ds(res * seg + start, size), :] for res in range(DIL_PERM)], axis=0)

    for i in range(seq // n):
        q = gather(q_ref, sub * i, sub)
        if i == 0:
            res3 = attend(q, gather(k_ref, 0, sub), gather(vo, 0, sub), bias_first)
        else:
            res3 = attend(q, gather(k_ref, sub * (i - 1), 2 * sub), gather(vo, sub * (i - 1), 2 * sub), bias_prev)
        for res in range(DIL_PERM):
            put(0, pl.ds(res * seg + sub * i, sub), tuple(t[res * sub:(res + 1) * sub] for t in res3))

    r = DIL_PAIRS[1][1]
    assert r == DIL_PERM
    a, c = iota2(2 * n)
    bias_prev = bias_of(r * (a - c + n), DIL_PAIRS[1][0])
    bias_first = bias_prev[:, n:]
    r16 = DIL_PAIRS[2][1]
    step = r16 // DIL_PERM
    assert seq // r16 == n
    a, c = iota2(n)
    bias_16 = bias_of(r16 * (a - c), DIL_PAIRS[2][0])
    chunk = _pick(seg, ROW_CHUNK)

    for res in range(DIL_PERM):
        grp = pl.ds(res * seg, seg)
        for i in range(seg // n):
            row0 = res * seg + i * n
            q = q_ref[pl.ds(row0, n), :]
            if i == 0:
                res3 = attend(q, k_ref[pl.ds(row0, n), :], vo[pl.ds(row0, n), :], bias_first)
            else:
                res3 = attend(q, k_ref[pl.ds(row0 - n, 2 * n), :], vo[pl.ds(row0 - n, 2 * n), :], bias_prev)
            put(1, pl.ds(row0, n), res3)

        qf[grp, :] = q_ref[grp, :].astype(F32)
        kf[grp, :] = k_ref[grp, :].astype(F32)
        vf[grp, :] = v_ref[grp, :].astype(F32)
        for sres in range(step):
            src = pl.ds(res * seg + sres, n, stride=step)
            blk = pl.ds((res * step + sres) * n, n)
            q16[blk, :] = qf[src, :].astype(BF16)
            k16[blk, :] = kf[src, :].astype(BF16)
            vo16[blk, :HEAD_DIM] = vf[src, :].astype(BF16)
            put(2, src, attend(q16[blk, :], k16[blk, :], vo16[blk, :], bias_16))

        for l0 in range(0, seg, chunk):
            rows = pl.ds(res * seg + l0, chunk)
            m_all = jnp.maximum(jnp.maximum(max_s[0, rows, :], max_s[1, rows, :]), max_s[2, rows, :])
            num = jnp.zeros((chunk, HEAD_DIM), F32)
            den = jnp.zeros((chunk, HEAD_DIM), F32)
            for br in range(len(DIL_PAIRS)):
                w = jnp.exp2(max_s[br, rows, :] - m_all)
                num = num + num_s[br, rows, :] * w
                den = den + den_s[br, rows, :] * w
            onat[pl.ds(DIL_PERM * l0 + res, chunk, stride=DIL_PERM), :] = _head_epilogue(
                num / den, g_ref[...], z_ref[rows, :])


def _dil_group_kernel(*refs, seq, heads):
    per_head = 6
    o_ref, onat = refs[per_head * heads:per_head * heads + 2]
    scratch = refs[per_head * heads + 2:]
    n_scr = len(scratch) // heads
    for hh in range(heads):
        ins = refs[per_head * hh:per_head * (hh + 1)]
        _dil_kernel(*ins, onat.at[hh], *scratch[n_scr * hh:n_scr * (hh + 1)], seq=seq)
    chunk = _pick(seq, ROW_CHUNK)
    for c0 in range(0, seq, chunk):
        rows = pl.ds(c0, chunk)
        o_ref[rows, :] = jnp.concatenate([onat[hh, rows, :] for hh in range(heads)], axis=1).astype(BF16)


def _dil_attention(proj, g_dil, slopes, n_heads):
    b, s, _ = proj.shape
    nbr = len(DIL_PAIRS)
    heads = DIL_HEADS_PER_STEP
    width = heads * HEAD_DIM
    groups = n_heads // heads
    assert n_heads % heads == 0
    rows_f32 = pltpu.VMEM((s, HEAD_DIM), F32)
    rows_bf16 = pltpu.VMEM((s, HEAD_DIM), BF16)
    rows2_bf16 = pltpu.VMEM((s, 2 * HEAD_DIM), BF16)
    per_branch = pltpu.VMEM((nbr, s, HEAD_DIM), F32)

    in_specs, operands = [], []
    for hh in range(heads):
        def head(h, hh=hh):
            return heads * h + hh

        for off in range(4):
            in_specs.append(pl.BlockSpec((None, s, HEAD_DIM),
                                         lambda i, h, off=off, head=head: (i, 0, off * n_heads + head(h))))
        in_specs.append(pl.BlockSpec((1, HEAD_DIM), lambda i, h, head=head: (0, head(h))))
        in_specs.append(pl.BlockSpec((None, 1, HEAD_DIM), lambda i, h, head=head: (head(h), 0, 0)))
        operands += [proj, proj, proj, proj, g_dil, slopes]

    return pl.pallas_call(
        functools.partial(_dil_group_kernel, seq=s, heads=heads),
        grid=(b, groups),
        in_specs=in_specs,
        out_specs=pl.BlockSpec((None, s, width), lambda i, h: (i, 0, h)),
        out_shape=jax.ShapeDtypeStruct((b, s, n_heads * HEAD_DIM), BF16),
        scratch_shapes=[pltpu.VMEM((heads, s, HEAD_DIM), F32)]
        + heads * [rows_f32, rows_f32, rows_f32, rows_bf16, rows_bf16, rows2_bf16, rows2_bf16,
                   per_branch, per_branch, per_branch],
        compiler_params=_params("arbitrary", "arbitrary"),
        name="dil_attn",
    )(*operands)


def _outproj_kernel(ya_ref, yb_ref, wk_ref, x_ref, gate_ref, o_ref, wb_ref):
    j = pl.program_id(0)
    i = pl.program_id(1)
    kc = wk_ref.shape[0]
    ka = ya_ref.shape[1]

    def cast_chunk():
        wb_ref[j % 2, pl.ds(pl.multiple_of(i * kc, kc), kc), :] = wk_ref[...].astype(BF16)

    @pl.when(j == 0)
    def _():
        cast_chunk()

    @pl.when(j > 0)
    def _():
        cast_chunk()
        cur = (j - 1) % 2
        acc = jnp.dot(ya_ref[...], wb_ref[cur, :ka, :], preferred_element_type=F32)
        acc = acc + jnp.dot(yb_ref[...], wb_ref[cur, ka:, :], preferred_element_type=F32)
        o_ref[...] = x_ref[...] + gate_ref[...] * acc


def _outproj(y_sb, y_dl, w_out, x, gate):
    b, s, d = x.shape
    ka = y_sb.shape[2]
    kb = y_dl.shape[2]
    tm = _pick(s, OUT_PROJ_TM)
    tn = _pick(d, OUT_PROJ_TN)
    nm = s // tm
    steps = b * nm
    ncb = d // tn
    kc = (ka + kb) // steps
    assert (ka + kb) % steps == 0 and kc % BF16_ROWS_PER_VREG == 0

    def row(j, i):
        i = jnp.where(j == 0, 0, i)
        return i // nm, i % nm

    def col(j):
        return jnp.maximum(j - 1, 0)

    return pl.pallas_call(
        _outproj_kernel,
        grid=(ncb + 1, steps),
        in_specs=[
            pl.BlockSpec((None, tm, ka), lambda j, i: (*row(j, i), 0)),
            pl.BlockSpec((None, tm, kb), lambda j, i: (*row(j, i), 0)),
            pl.BlockSpec((kc, tn), lambda j, i: (i, jnp.minimum(j, ncb - 1))),
            pl.BlockSpec((None, tm, tn), lambda j, i: (*row(j, i), col(j))),
            pl.BlockSpec((None, 1, tn), lambda j, i: (row(j, i)[0], 0, col(j))),
        ],
        out_specs=pl.BlockSpec((None, tm, tn), lambda j, i: (*row(j, i), col(j))),
        out_shape=jax.ShapeDtypeStruct((b, s, d), F32),
        scratch_shapes=[pltpu.VMEM((2, ka + kb, tn), BF16)],
        compiler_params=pltpu.CompilerParams(dimension_semantics=("arbitrary", "arbitrary"),
                                             vmem_limit_bytes=OUT_PROJ_VMEM_LIMIT),
        name="out_proj",
    )(y_sb, y_dl, w_out, x, gate)


def _final_norm_kernel(x_ref, g_ref, o_ref):
    x = x_ref[...]
    ms = jnp.mean(x * x, axis=-1, keepdims=True)
    o_ref[...] = x * lax.rsqrt(ms + EPS) * g_ref[...]


def _final_norm(x, g):
    b, s, d = x.shape
    ts = _pick(s, NORM_ROWS)
    return pl.pallas_call(
        _final_norm_kernel,
        grid=(b, s // ts),
        in_specs=[pl.BlockSpec((None, ts, d), lambda i, j: (i, j, 0)),
                  pl.BlockSpec((1, d), lambda i, j: (0, 0))],
        out_specs=pl.BlockSpec((None, ts, d), lambda i, j: (i, j, 0)),
        out_shape=jax.ShapeDtypeStruct((b, s, d), F32),
        compiler_params=_params("arbitrary", "arbitrary"),
        name="final_norm",
    )(x, g)


def kernel(x, c, w_ada, b_ada, g_norm, w_in, g_sb, g_dil, w_out, g_final):
    b, s, d = x.shape
    depth = w_ada.shape[0]
    n_sb = g_sb.shape[1] // HEAD_DIM
    n_dil = g_dil.shape[1] // HEAD_DIM
    sb_w = n_sb * HEAD_DIM
    dil_w = n_dil * HEAD_DIM
    assert s % DIL_PAIRS[-1][0] == 0 and sb_w == dil_w

    qscale = LOG2E / math.sqrt(HEAD_DIM)
    colscale = jnp.concatenate([
        jnp.full((sb_w,), qscale, F32), jnp.ones((3 * sb_w,), F32),
        jnp.full((dil_w,), qscale, F32), jnp.ones((3 * dil_w,), F32)])[None, :]
    slopes = jnp.exp2(-ALIBI_MAX_BIAS * jnp.arange(1, n_dil + 1, dtype=F32) / n_dil) * LOG2E
    slopes = jnp.broadcast_to(slopes[:, None, None], (n_dil, 1, HEAD_DIM))

    assert b <= BF16_ROWS_PER_VREG
    c_pad = jnp.zeros((BF16_ROWS_PER_VREG, d), F32).at[:b].set(c.astype(F32))
    for layer in range(depth):
        mod = _ada(c_pad, w_ada[layer], b_ada[layer][None, :])[:b]
        shift, scale, gate = (mod[:, i * d:(i + 1) * d][:, None, :] for i in range(3))
        h = _norm_mod(x, g_norm[layer][None, :], scale, shift)
        proj_sb = _inproj(h, w_in[layer], colscale, 0, 4 * sb_w, 1)
        proj_dl = _inproj(h, w_in[layer], colscale, 4 * sb_w, 4 * dil_w, DIL_PERM)
        y_sb = _sb_attention(proj_sb, g_sb[layer][None, :], n_sb)
        y_dl = _dil_attention(proj_dl, g_dil[layer][None, :], slopes, n_dil)
        x = _outproj(y_sb, y_dl, w_out[layer], x, gate)
    return _final_norm(x, g_final[None, :])
```

```python
import functools
import math

import jax
import jax.numpy as jnp
from jax import lax
from jax.experimental import pallas as pl
from jax.experimental.pallas import tpu as pltpu

HEAD_DIM = 128
EPS = 1e-6
LOG2E = 1.4426950408889634
ALIBI_MAX_BIAS = 8.0
DIL_PAIRS = ((128, 1), (512, 4), (2048, 16))
DIL_BLOCK = 128
DIL_PERM = 4
MASK_BIAS = -1e30
V7X_VMEM_BYTES = 64 * 1024 * 1024
V7X_MXU_DIM = 256
BF16_ROWS_PER_VREG = 16
VMEM_LIMIT = V7X_VMEM_BYTES - 8 * 1024 * 1024
IN_PROJ_TM, IN_PROJ_TN = 1024, 1024
OUT_PROJ_TM, OUT_PROJ_TN = 1024, 1024
OUT_PROJ_VMEM_LIMIT = V7X_VMEM_BYTES - 2 * 1024 * 1024
ADA_TN = 512
NORM_ROWS = 512
NORM_CHUNK = BF16_ROWS_PER_VREG
ROW_CHUNK = V7X_MXU_DIM
DIL_HEADS_PER_STEP = 2

F32 = jnp.float32
BF16 = jnp.bfloat16


def _params(*sem):
    return pltpu.CompilerParams(dimension_semantics=sem, vmem_limit_bytes=VMEM_LIMIT)


def _pick(n, pref):
    t = min(pref, n)
    while n % t:
        t //= 2
    return t


def _silu(x):
    return x / (1.0 + jnp.exp(-x))


def _dot_nt(a, b):
    return lax.dot_general(a, b, (((1,), (1,)), ((), ())), preferred_element_type=F32)


def _ada_kernel(c_ref, w_ref, b_ref, o_ref):
    cs = _silu(c_ref[...])
    acc = jnp.dot(cs.astype(BF16), w_ref[...].astype(BF16), preferred_element_type=F32)
    o_ref[...] = acc + b_ref[...]


def _ada(c_pad, w, b):
    m, d = c_pad.shape
    n = w.shape[1]
    tn = _pick(n, ADA_TN)
    return pl.pallas_call(
        _ada_kernel,
        grid=(n // tn,),
        in_specs=[
            pl.BlockSpec((m, d), lambda j: (0, 0)),
            pl.BlockSpec((d, tn), lambda j: (0, j)),
            pl.BlockSpec((1, tn), lambda j: (0, j)),
        ],
        out_specs=pl.BlockSpec((m, tn), lambda j: (0, j)),
        out_shape=jax.ShapeDtypeStruct((m, n), F32),
        compiler_params=_params("arbitrary"),
        name="ada_ln",
    )(c_pad, w, b)


def _norm_mod_kernel(x_ref, g_ref, scale_ref, shift_ref, h_ref, gs_ref):
    @pl.when(pl.program_id(1) == 0)
    def _():
        gs_ref[...] = g_ref[...] * (1.0 + scale_ref[...])

    def chunk(i, _):
        rows = pl.ds(pl.multiple_of(i * NORM_CHUNK, NORM_CHUNK), NORM_CHUNK)
        x = x_ref[rows, :]
        ms = jnp.mean(x * x, axis=-1, keepdims=True)
        h_ref[rows, :] = (x * lax.rsqrt(ms + EPS) * gs_ref[...] + shift_ref[...]).astype(BF16)
        return 0

    lax.fori_loop(0, x_ref.shape[0] // NORM_CHUNK, chunk, 0)


def _norm_mod(x, g, scale, shift):
    b, s, d = x.shape
    ts = _pick(s, NORM_ROWS)
    return pl.pallas_call(
        _norm_mod_kernel,
        grid=(b, s // ts),
        in_specs=[
            pl.BlockSpec((None, ts, d), lambda i, j: (i, j, 0)),
            pl.BlockSpec((1, d), lambda i, j: (0, 0)),
            pl.BlockSpec((None, 1, d), lambda i, j: (i, 0, 0)),
            pl.BlockSpec((None, 1, d), lambda i, j: (i, 0, 0)),
        ],
        out_specs=pl.BlockSpec((None, ts, d), lambda i, j: (i, j, 0)),
        out_shape=jax.ShapeDtypeStruct((b, s, d), BF16),
        scratch_shapes=[pltpu.VMEM((1, d), F32)],
        compiler_params=_params("arbitrary", "arbitrary"),
        name="norm_mod",
    )(x, g, scale, shift)


def _inproj_kernel(h_ref, wk_ref, cs_ref, o_ref, wb_ref, *maybe_slabs, perm):
    j = pl.program_id(0)
    i = pl.program_id(1)
    kc = wk_ref.shape[0]

    def cast_chunk():
        rows = pl.ds(pl.multiple_of(i * kc, kc), kc)
        wb_ref[j % 2, rows, :] = (wk_ref[...] * cs_ref[...]).astype(BF16)

    @pl.when(j == 0)
    def _():
        cast_chunk()

    @pl.when(j > 0)
    def _():
        cast_chunk()
        vals = jnp.dot(h_ref[...], wb_ref[(j - 1) % 2], preferred_element_type=F32)
        if perm == 1:
            o_ref[...] = vals.astype(BF16)
        else:
            (slab_ref,) = maybe_slabs
            tm, tn = vals.shape
            lanes = slab_ref.shape[2]
            for slab in range(tn // lanes):
                slab_ref[slab] = vals[:, slab * lanes:(slab + 1) * lanes]
            for res in range(perm):
                for slab in range(tn // lanes):
                    rows = slab_ref[slab, pl.ds(res, tm // perm, stride=perm), :]
                    o_ref[res, :, slab * lanes:(slab + 1) * lanes] = rows.astype(BF16)


def _inproj(h, w, colscale, col0, ncols, perm):
    b, s, d = h.shape
    tm = _pick(s, IN_PROJ_TM)
    tn = _pick(ncols, IN_PROJ_TN)
    nm = s // tm
    steps = b * nm
    ncb = ncols // tn
    joff = col0 // tn
    kc = d // steps
    assert col0 % tn == 0 and tm % (BF16_ROWS_PER_VREG * perm) == 0
    assert d % steps == 0 and kc % BF16_ROWS_PER_VREG == 0

    def row(j, i):
        i = jnp.where(j == 0, 0, i)
        return i // nm, i % nm

    def next_col(j):
        return joff + jnp.minimum(j, ncb - 1)

    in_specs = [
        pl.BlockSpec((None, tm, d), lambda j, i: (*row(j, i), 0)),
        pl.BlockSpec((kc, tn), lambda j, i: (i, next_col(j))),
        pl.BlockSpec((1, tn), lambda j, i: (0, next_col(j))),
    ]
    scratch = [pltpu.VMEM((2, d, tn), BF16)]
    if perm == 1:
        out_spec = pl.BlockSpec((None, tm, tn), lambda j, i: (*row(j, i), jnp.maximum(j - 1, 0)))
        out_shape = (b, s, ncols)
    else:
        out_spec = pl.BlockSpec((None, perm, tm // perm, tn),
                                lambda j, i: (row(j, i)[0], 0, row(j, i)[1], jnp.maximum(j - 1, 0)))
        out_shape = (b, perm, s // perm, ncols)
        scratch.append(pltpu.VMEM((tn // HEAD_DIM, tm, HEAD_DIM), F32))
    out = pl.pallas_call(
        functools.partial(_inproj_kernel, perm=perm),
        grid=(ncb + 1, steps),
        in_specs=in_specs,
        out_specs=out_spec,
        out_shape=jax.ShapeDtypeStruct(out_shape, BF16),
        scratch_shapes=scratch,
        compiler_params=_params("arbitrary", "arbitrary"),
        name="in_proj_p%d" % perm,
    )(h, w, colscale)
    return out.reshape(b, s, ncols)


def _head_epilogue(o, g, zg):
    ms = jnp.mean(o * o, axis=-1, keepdims=True)
    return o * lax.rsqrt(ms + EPS) * g * _silu(zg.astype(F32))


def _sb_kernel(q_ref, k_ref, v_ref, z_ref, g_ref, o_ref, acc_ref, carry_ref, lbc_all, lom_all,
               *, seq, tile):
    row = lax.broadcasted_iota(jnp.int32, (tile, tile), 0)
    col = lax.broadcasted_iota(jnp.int32, (tile, tile), 1)
    causal = col < row
    upper = jnp.where(row > col, 1.0, 0.0).astype(BF16)

    def logs(kj, rows, carry, mask):
        s = _dot_nt(q_ref[rows, :], k_ref[pl.ds(kj * tile, tile), :])
        e = jnp.exp2(-jnp.abs(s))
        lb = jnp.minimum(s, 0.0) - jnp.log(1.0 + e) * LOG2E
        lom = lb - s
        if mask is not None:
            lom = jnp.where(mask, lom, 0.0)
        lom_all[kj, rows, :] = lom.astype(BF16)
        lbc_all[kj, rows, :] = lb if carry is None else lb + carry
        return jnp.sum(lom, axis=1, keepdims=True)

    def weights_times_v(kj, rows, mask):
        arg = lbc_all[kj, rows, :] + jnp.dot(lom_all[kj, rows, :], upper, preferred_element_type=F32)
        a = jnp.exp2(arg)
        if mask is not None:
            a = jnp.where(mask, a, 0.0)
        return jnp.dot(a.astype(BF16), v_ref[pl.ds(kj * tile, tile), :], preferred_element_type=F32)

    def pass1(kj):
        k0 = kj * tile
        first = pl.ds(k0, tile)
        carry_ref[first, :] = logs(kj, first, None, causal)
        if k0 + tile < seq:
            later = pl.ds(k0 + tile, seq - k0 - tile)
            carry_ref[later, :] += logs(kj, later, carry_ref[later, :], None)

    def pass2(kj):
        k0 = kj * tile
        first = pl.ds(k0, tile)
        acc_ref[first, :] = weights_times_v(kj, first, causal)
        if k0 + tile < seq:
            later = pl.ds(k0 + tile, seq - k0 - tile)
            acc_ref[later, :] += weights_times_v(kj, later, None)

    for kj in reversed(range(seq // tile)):
        pass1(kj)
    for kj in reversed(range(seq // tile)):
        pass2(kj)

    for q0 in range(0, seq, tile):
        rows = pl.ds(q0, tile)
        o_ref[rows, :] = _head_epilogue(acc_ref[rows, :], g_ref[...], z_ref[rows, :]).astype(BF16)


def _head_spec(s, n_heads, off):
    return pl.BlockSpec((None, s, HEAD_DIM), lambda i, h: (i, 0, off * n_heads + h))


def _sb_attention(proj, g_sb, n_heads):
    b, s, _ = proj.shape
    tile = _pick(s, V7X_MXU_DIM)
    return pl.pallas_call(
        functools.partial(_sb_kernel, seq=s, tile=tile),
        grid=(b, n_heads),
        in_specs=[_head_spec(s, n_heads, 0), _head_spec(s, n_heads, 1), _head_spec(s, n_heads, 2),
                  _head_spec(s, n_heads, 3), pl.BlockSpec((1, HEAD_DIM), lambda i, h: (0, h))],
        out_specs=pl.BlockSpec((None, s, HEAD_DIM), lambda i, h: (i, 0, h)),
        out_shape=jax.ShapeDtypeStruct((b, s, n_heads * HEAD_DIM), BF16),
        scratch_shapes=[pltpu.VMEM((s, HEAD_DIM), F32), pltpu.VMEM((s, 1), F32),
                        pltpu.VMEM((s // tile, s, tile), F32), pltpu.VMEM((s // tile, s, tile), BF16)],
        compiler_params=_params("arbitrary", "arbitrary"),
        name="sb_attn",
    )(proj, proj, proj, proj, g_sb)


def _dil_kernel(q_ref, k_ref, v_ref, z_ref, g_ref, slope_ref, onat,
                qf, kf, vf, q16, k16, vo, vo16, num_s, den_s, max_s, *, seq):
    n = DIL_BLOCK
    seg = seq // DIL_PERM
    sub = n // DIL_PERM
    slope = slope_ref[0:1, 0:1]

    def bias_of(dist, window):
        return jnp.where((dist >= 0) & (dist <= window), dist.astype(F32) * (-slope), MASK_BIAS)

    def iota2(nk):
        return (lax.broadcasted_iota(jnp.int32, (n, nk), 0), lax.broadcasted_iota(jnp.int32, (n, nk), 1))

    def attend(q, k, vv, bias):
        s = _dot_nt(q, k) + bias
        m = jnp.max(s, axis=1, keepdims=True)
        p = jnp.exp2(s - m)
        pv = jnp.dot(p.astype(BF16), vv, preferred_element_type=F32)
        return pv[:, :HEAD_DIM], pv[:, HEAD_DIM:], jnp.broadcast_to(m, (n, HEAD_DIM))

    def put(br, idx, res):
        num_s[br, idx, :], den_s[br, idx, :], max_s[br, idx, :] = res

    ones = jnp.ones((seq, HEAD_DIM), BF16)
    vo[:, :HEAD_DIM] = v_ref[...]
    vo[:, HEAD_DIM:] = ones
    vo16[:, HEAD_DIM:] = ones

    a, c = iota2(2 * n)
    tq = (a % sub) * DIL_PERM + a // sub
    bias_prev = bias_of(tq - ((c % (2 * sub)) * DIL_PERM + c // (2 * sub) - n), DIL_PAIRS[0][0])
    a, c = iota2(n)
    bias_first = bias_of(tq[:, :n] - ((c % sub) * DIL_PERM + c // sub), DIL_PAIRS[0][0])

    def gather(ref, start, size):
        return jnp.concatenate([ref[pl.ds(res * seg + start, size), :] for res in range(DIL_PERM)], axis=0)

    for i in range(seq // n):
        q = gather(q_ref, sub * i, sub)
        if i == 0:
            res3 = attend(q, gather(k_ref, 0, sub), gather(vo, 0, sub), bias_first)
        else:
            res3 = attend(q, gather(k_ref, sub * (i - 1), 2 * sub), gather(vo, sub * (i - 1), 2 * sub), bias_prev)
        for res in range(DIL_PERM):
            put(0, pl.ds(res * seg + sub * i, sub), tuple(t[res * sub:(res + 1) * sub] for t in res3))

    r = DIL_PAIRS[1][1]
    assert r == DIL_PERM
    a, c = iota2(2 * n)
    bias_prev = bias_of(r * (a - c + n), DIL_PAIRS[1][0])
    bias_first = bias_prev[:, n:]
    r16 = DIL_PAIRS[2][1]
    step = r16 // DIL_PERM
    assert seq // r16 == n
    a, c = iota2(n)
    bias_16 = bias_of(r16 * (a - c), DIL_PAIRS[2][0])
    chunk = _pick(seg, ROW_CHUNK)

    for res in range(DIL_PERM):
        grp = pl.ds(res * seg, seg)
        for i in range(seg // n):
            row0 = res * seg + i * n
            q = q_ref[pl.ds(row0, n), :]
            if i == 0:
                res3 = attend(q, k_ref[pl.ds(row0, n), :], vo[pl.ds(row0, n), :], bias_first)
            else:
                res3 = attend(q, k_ref[pl.ds(row0 - n, 2 * n), :], vo[pl.ds(row0 - n, 2 * n), :], bias_prev)
            put(1, pl.ds(row0, n), res3)

        qf[grp, :] = q_ref[grp, :].astype(F32)
        kf[grp, :] = k_ref[grp, :].astype(F32)
        vf[grp, :] = v_ref[grp, :].astype(F32)
        for sres in range(step):
            src = pl.ds(res * seg + sres, n, stride=step)
            blk = pl.ds((res * step + sres) * n, n)
            q16[blk, :] = qf[src, :].astype(BF16)
            k16[blk, :] = kf[src, :].astype(BF16)
            vo16[blk, :HEAD_DIM] = vf[src, :].astype(BF16)
            put(2, src, attend(q16[blk, :], k16[blk, :], vo16[blk, :], bias_16))

        for l0 in range(0, seg, chunk):
            rows = pl.ds(res * seg + l0, chunk)
            m_all = jnp.maximum(jnp.maximum(max_s[0, rows, :], max_s[1, rows, :]), max_s[2, rows, :])
            num = jnp.zeros((chunk, HEAD_DIM), F32)
            den = jnp.zeros((chunk, HEAD_DIM), F32)
            for br in range(len(DIL_PAIRS)):
                w = jnp.exp2(max_s[br, rows, :] - m_all)
                num = num + num_s[br, rows, :] * w
                den = den + den_s[br, rows, :] * w
            onat[pl.ds(DIL_PERM * l0 + res, chunk, stride=DIL_PERM), :] = _head_epilogue(
                num / den, g_ref[...], z_ref[rows, :])


def _dil_group_kernel(*refs, seq, heads):
    per_head = 6
    o_ref, onat = refs[per_head * heads:per_head * heads + 2]
    scratch = refs[per_head * heads + 2:]
    n_scr = len(scratch) // heads
    for hh in range(heads):
        ins = refs[per_head * hh:per_head * (hh + 1)]
        _dil_kernel(*ins, onat.at[hh], *scratch[n_scr * hh:n_scr * (hh + 1)], seq=seq)
    chunk = _pick(seq, ROW_CHUNK)
    for c0 in range(0, seq, chunk):
        rows = pl.ds(c0, chunk)
        o_ref[rows, :] = jnp.concatenate([onat[hh, rows, :] for hh in range(heads)], axis=1).astype(BF16)


def _dil_attention(proj, g_dil, slopes, n_heads):
    b, s, _ = proj.shape
    nbr = len(DIL_PAIRS)
    heads = DIL_HEADS_PER_STEP
    width = heads * HEAD_DIM
    groups = n_heads // heads
    assert n_heads % heads == 0
    rows_f32 = pltpu.VMEM((s, HEAD_DIM), F32)
    rows_bf16 = pltpu.VMEM((s, HEAD_DIM), BF16)
    rows2_bf16 = pltpu.VMEM((s, 2 * HEAD_DIM), BF16)
    per_branch = pltpu.VMEM((nbr, s, HEAD_DIM), F32)

    in_specs, operands = [], []
    for hh in range(heads):
        def head(h, hh=hh):
            return heads * h + hh

        for off in range(4):
            in_specs.append(pl.BlockSpec((None, s, HEAD_DIM),
                                         lambda i, h, off=off, head=head: (i, 0, off * n_heads + head(h))))
        in_specs.append(pl.BlockSpec((1, HEAD_DIM), lambda i, h, head=head: (0, head(h))))
        in_specs.append(pl.BlockSpec((None, 1, HEAD_DIM), lambda i, h, head=head: (head(h), 0, 0)))
        operands += [proj, proj, proj, proj, g_dil, slopes]

    return pl.pallas_call(
        functools.partial(_dil_group_kernel, seq=s, heads=heads),
        grid=(b, groups),
        in_specs=in_specs,
        out_specs=pl.BlockSpec((None, s, width), lambda i, h: (i, 0, h)),
        out_shape=jax.ShapeDtypeStruct((b, s, n_heads * HEAD_DIM), BF16),
        scratch_shapes=[pltpu.VMEM((heads, s, HEAD_DIM), F32)]
        + heads * [rows_f32, rows_f32, rows_f32, rows_bf16, rows_bf16, rows2_bf16, rows2_bf16,
                   per_branch, per_branch, per_branch],
        compiler_params=_params("arbitrary", "arbitrary"),
        name="dil_attn",
    )(*operands)


def _outproj_kernel(ya_ref, yb_ref, wk_ref, x_ref, gate_ref, o_ref, wb_ref):
    j = pl.program_id(0)
    i = pl.program_id(1)
    kc = wk_ref.shape[0]
    ka = ya_ref.shape[1]

    def cast_chunk():
        wb_ref[j % 2, pl.ds(pl.multiple_of(i * kc, kc), kc), :] = wk_ref[...].astype(BF16)

    @pl.when(j == 0)
    def _():
        cast_chunk()

    @pl.when(j > 0)
    def _():
        cast_chunk()
        cur = (j - 1) % 2
        acc = jnp.dot(ya_ref[...], wb_ref[cur, :ka, :], preferred_element_type=F32)
        acc = acc + jnp.dot(yb_ref[...], wb_ref[cur, ka:, :], preferred_element_type=F32)
        o_ref[...] = x_ref[...] + gate_ref[...] * acc


def _outproj(y_sb, y_dl, w_out, x, gate):
    b, s, d = x.shape
    ka = y_sb.shape[2]
    kb = y_dl.shape[2]
    tm = _pick(s, OUT_PROJ_TM)
    tn = _pick(d, OUT_PROJ_TN)
    nm = s // tm
    steps = b * nm
    ncb = d // tn
    kc = (ka + kb) // steps
    assert (ka + kb) % steps == 0 and kc % BF16_ROWS_PER_VREG == 0

    def row(j, i):
        i = jnp.where(j == 0, 0, i)
        return i // nm, i % nm

    def col(j):
        return jnp.maximum(j - 1, 0)

    return pl.pallas_call(
        _outproj_kernel,
        grid=(ncb + 1, steps),
        in_specs=[
            pl.BlockSpec((None, tm, ka), lambda j, i: (*row(j, i), 0)),
            pl.BlockSpec((None, tm, kb), lambda j, i: (*row(j, i), 0)),
            pl.BlockSpec((kc, tn), lambda j, i: (i, jnp.minimum(j, ncb - 1))),
            pl.BlockSpec((None, tm, tn), lambda j, i: (*row(j, i), col(j))),
            pl.BlockSpec((None, 1, tn), lambda j, i: (row(j, i)[0], 0, col(j))),
        ],
        out_specs=pl.BlockSpec((None, tm, tn), lambda j, i: (*row(j, i), col(j))),
        out_shape=jax.ShapeDtypeStruct((b, s, d), F32),
        scratch_shapes=[pltpu.VMEM((2, ka + kb, tn), BF16)],
        compiler_params=pltpu.CompilerParams(dimension_semantics=("arbitrary", "arbitrary"),
                                             vmem_limit_bytes=OUT_PROJ_VMEM_LIMIT),
        name="out_proj",
    )(y_sb, y_dl, w_out, x, gate)


def _final_norm_kernel(x_ref, g_ref, o_ref):
    def chunk(i, _):
        rows = pl.ds(pl.multiple_of(i * NORM_CHUNK, NORM_CHUNK), NORM_CHUNK)
        x = x_ref[rows, :]
        ms = jnp.mean(x * x, axis=-1, keepdims=True)
        o_ref[rows, :] = x * lax.rsqrt(ms + EPS) * g_ref[...]
        return 0

    lax.fori_loop(0, x_ref.shape[0] // NORM_CHUNK, chunk, 0)


def _final_norm(x, g):
    b, s, d = x.shape
    ts = _pick(s, NORM_ROWS)
    return pl.pallas_call(
        _final_norm_kernel,
        grid=(b, s // ts),
        in_specs=[pl.BlockSpec((None, ts, d), lambda i, j: (i, j, 0)),
                  pl.BlockSpec((1, d), lambda i, j: (0, 0))],
        out_specs=pl.BlockSpec((None, ts, d), lambda i, j: (i, j, 0)),
        out_shape=jax.ShapeDtypeStruct((b, s, d), F32),
        compiler_params=_params("arbitrary", "arbitrary"),
        name="final_norm",
    )(x, g)


def kernel(x, c, w_ada, b_ada, g_norm, w_in, g_sb, g_dil, w_out, g_final):
    b, s, d = x.shape
    depth = w_ada.shape[0]
    n_sb = g_sb.shape[1] // HEAD_DIM
    n_dil = g_dil.shape[1] // HEAD_DIM
    sb_w = n_sb * HEAD_DIM
    dil_w = n_dil * HEAD_DIM
    assert s % DIL_PAIRS[-1][0] == 0 and sb_w == dil_w

    qscale = LOG2E / math.sqrt(HEAD_DIM)
    colscale = jnp.concatenate([
        jnp.full((sb_w,), qscale, F32), jnp.ones((3 * sb_w,), F32),
        jnp.full((dil_w,), qscale, F32), jnp.ones((3 * dil_w,), F32)])[None, :]
    slopes = jnp.exp2(-ALIBI_MAX_BIAS * jnp.arange(1, n_dil + 1, dtype=F32) / n_dil) * LOG2E
    slopes = jnp.broadcast_to(slopes[:, None, None], (n_dil, 1, HEAD_DIM))

    assert b <= BF16_ROWS_PER_VREG
    c_pad = jnp.zeros((BF16_ROWS_PER_VREG, d), F32).at[:b].set(c.astype(F32))
    for layer in range(depth):
        mod = _ada(c_pad, w_ada[layer], b_ada[layer][None, :])[:b]
        shift, scale, gate = (mod[:, i * d:(i + 1) * d][:, None, :] for i in range(3))
        h = _norm_mod(x, g_norm[layer][None, :], scale, shift)
        proj_sb = _inproj(h, w_in[layer], colscale, 0, 4 * sb_w, 1)
        proj_dl = _inproj(h, w_in[layer], colscale, 4 * sb_w, 4 * dil_w, DIL_PERM)
        y_sb = _sb_attention(proj_sb, g_sb[layer][None, :], n_sb)
        y_dl = _dil_attention(proj_dl, g_dil[layer][None, :], slopes, n_dil)
        x = _outproj(y_sb, y_dl, w_out[layer], x, gate)
    return _final_norm(x, g_final[None, :])
```
